```python
import math
import numpy as np
import jax
import jax.numpy as jnp
from jax import lax

D_MODEL = 1024
BATCH = 2
SEQ = 8192
DEPTH = 2

GRID_W = 64
CTX_LEN = 256
ROPE_BASE = 10000.0
EPS = 1e-6
Q_BLOCK = 128

MLA_HEADS = 8
MLA_Q_RANK = 256
MLA_KV_RANK = 128
MLA_NOPE = 64
MLA_ROPE = 32
MLA_V = 64
GQA_HEADS = 4
GQA_KV_HEADS = 2
GQA_HEAD_DIM = 64
CONV_CH = 256
CONV_WIDTH = 31

MLA_WIDTH = MLA_HEADS * MLA_V
GQA_WIDTH = GQA_HEADS * GQA_HEAD_DIM
MIX_WIDTH = MLA_WIDTH + GQA_WIDTH + CONV_CH
IN_SIZES = (MLA_Q_RANK, MLA_KV_RANK, MLA_ROPE, GQA_HEADS * GQA_HEAD_DIM,
            GQA_KV_HEADS * GQA_HEAD_DIM, GQA_KV_HEADS * GQA_HEAD_DIM, 2 * CONV_CH)
IN_WIDTH = (MLA_Q_RANK + MLA_KV_RANK + MLA_ROPE
            + (GQA_HEADS + 2 * GQA_KV_HEADS) * GQA_HEAD_DIM + 2 * CONV_CH)
MLA_SCALE = 1.0 / math.sqrt(MLA_NOPE + MLA_ROPE)
GQA_SCALE = 1.0 / math.sqrt(GQA_HEAD_DIM)

N_EXPERTS = 16
EXPERT_FF = 2688
CAPACITY_FACTOR = 2

kernel_name = "hybrid_mla_gqa_conformer_ecmoe_diffusion"


def rmsnorm(x, g):
    xf = x.astype(jnp.float32)
    y = xf * lax.rsqrt(jnp.mean(xf * xf, axis=-1, keepdims=True) + EPS)
    return (y * g).astype(x.dtype)


def layernorm(x, g, b):
    xf = x.astype(jnp.float32)
    mu = jnp.mean(xf, axis=-1, keepdims=True)
    var = jnp.mean(jnp.square(xf - mu), axis=-1, keepdims=True)
    return ((xf - mu) * lax.rsqrt(var + EPS) * g + b).astype(x.dtype)


def modulate(h, shift, scale):
    return h * (1.0 + scale) + shift


def axial_rope_tables(n, d_rot):
    rows = n // GRID_W
    row = jnp.repeat(jnp.arange(rows), GRID_W).astype(jnp.float32)
    col = jnp.tile(jnp.arange(GRID_W), rows).astype(jnp.float32)
    d_axis = d_rot // 2
    inv = ROPE_BASE ** (-jnp.arange(0, d_axis, 2, dtype=jnp.float32) / d_axis)
    ang = jnp.concatenate([row[:, None] * inv, col[:, None] * inv], axis=-1)
    return (jnp.cos(ang), jnp.sin(ang))


def apply_rope(x, cos, sin):
    xf = x.astype(jnp.float32)
    half = x.shape[-1] // 2
    x1, x2 = xf[..., :half], xf[..., half:]
    cs, sn = cos[None, :, None, :], sin[None, :, None, :]
    return jnp.concatenate([x1 * cs - x2 * sn, x2 * cs + x1 * sn], axis=-1).astype(x.dtype)


def attend(q, k, v, scale):
    B, n, G, R, dk = q.shape
    nblk = n // Q_BLOCK
    qb = q.reshape(B, nblk, Q_BLOCK, G, R, dk).transpose(1, 0, 2, 3, 4, 5)

    def one_block(q_blk):
        s = jnp.einsum("bqgrd,bkgd->bgrqk", q_blk, k, preferred_element_type=jnp.float32) * scale
        pr = jax.nn.softmax(s, axis=-1).astype(v.dtype)
        return jnp.einsum("bgrqk,bkge->bqgre", pr, v)

    o = lax.map(one_block, qb)
    return o.transpose(1, 0, 2, 3, 4, 5).reshape(B, n, G, R, v.shape[-1])


def split_in(z):
    offsets = np.cumsum(IN_SIZES)[:-1].tolist()
    return jnp.split(z, offsets, axis=-1)


def project_side(h, p, tabs, with_queries):
    B, n, _ = h.shape
    cq, ckv, kr, zq, zk, zv, zc = split_in(h @ p["w_in"])
    kv = (rmsnorm(ckv, p["mla_kv_norm_g"]) @ p["mla_w_ukv"]).reshape(B, n, MLA_HEADS, MLA_NOPE + MLA_V)
    k_nope, v_mla = kv[..., :MLA_NOPE], kv[..., MLA_NOPE:]
    k_rope = kr[:, :, None, :]
    gk = rmsnorm(zk.reshape(B, n, GQA_KV_HEADS, GQA_HEAD_DIM), p["gqa_k_norm_g"])
    gv = zv.reshape(B, n, GQA_KV_HEADS, GQA_HEAD_DIM)
    if tabs is not None:
        cos_m, sin_m, cos_g, sin_g = tabs
        k_rope = apply_rope(k_rope, cos_m, sin_m)
        gk = apply_rope(gk, cos_g, sin_g)
    k_mla = jnp.concatenate([k_nope, jnp.broadcast_to(k_rope, (B, n, MLA_HEADS, MLA_ROPE))], axis=-1)
    out = {"k_mla": k_mla, "v_mla": v_mla, "k_gqa": gk, "v_gqa": gv}
    if with_queries:
        q = (rmsnorm(cq, p["mla_q_norm_g"]) @ p["mla_w_uq"]).reshape(B, n, MLA_HEADS, MLA_NOPE + MLA_ROPE)
        q_nope, q_rope = q[..., :MLA_NOPE], q[..., MLA_NOPE:]
        gq = rmsnorm(zq.reshape(B, n, GQA_HEADS, GQA_HEAD_DIM), p["gqa_q_norm_g"])
        if tabs is not None:
            q_rope = apply_rope(q_rope, cos_m, sin_m)
            gq = apply_rope(gq, cos_g, sin_g)
        out["q_mla"] = jnp.concatenate([q_nope, q_rope], axis=-1)[:, :, :, None, :]
        out["q_gqa"] = gq.reshape(B, n, GQA_KV_HEADS, GQA_HEADS // GQA_KV_HEADS, GQA_HEAD_DIM)
        out["conv_in"] = zc
    return out


def conformer_conv(zc, p):
    a, b = jnp.split(zc, 2, axis=-1)
    u = a * jax.nn.sigmoid(b)
    y = lax.conv_general_dilated(
        u, p["conv_dw_w"][:, None, :], window_strides=(1,),
        padding=[(CONV_WIDTH // 2, CONV_WIDTH // 2)],
        dimension_numbers=("NWC", "WIO", "NWC"), feature_group_count=CONV_CH) + p["conv_dw_b"]
    return jax.nn.silu(layernorm(y, p["conv_ln_g"], p["conv_ln_b"]))


def mix_side(side, k_mla, v_mla, k_gqa, v_gqa, p):
    B, n = side["q_mla"].shape[:2]
    o_mla = attend(side["q_mla"], k_mla, v_mla, MLA_SCALE).reshape(B, n, MLA_WIDTH)
    o_gqa = attend(side["q_gqa"], k_gqa, v_gqa, GQA_SCALE).reshape(B, n, GQA_WIDTH)
    o_conv = conformer_conv(side["conv_in"], p)
    g = p["mix_out_g"]
    cat = jnp.concatenate([
        rmsnorm(o_mla, g[:MLA_WIDTH]),
        rmsnorm(o_gqa, g[MLA_WIDTH:MLA_WIDTH + GQA_WIDTH]),
        rmsnorm(o_conv, g[MLA_WIDTH + GQA_WIDTH:]),
    ], axis=-1)
    return cat @ p["w_out"]


def expert_choice_ffn(h, p):
    B, n, D = h.shape
    cap = CAPACITY_FACTOR * n // N_EXPERTS
    aff = jax.nn.softmax(jnp.einsum("bnd,de->bne", h, p["router_w"], preferred_element_type=jnp.float32), axis=-1)
    gate, idx = lax.top_k(aff.transpose(0, 2, 1), cap)
    xg = jax.vmap(lambda hb, ib: hb[ib])(h, idx)
    hid = jax.nn.silu(jnp.einsum("becd,edf->becf", xg, p["exp_w1"])) * jnp.einsum("becd,edf->becf", xg, p["exp_w3"])
    y = jnp.einsum("becf,efd->becd", hid, p["exp_w2"]) * gate[..., None].astype(h.dtype)
    return jax.vmap(lambda ib, yb: jnp.zeros((n, D), yb.dtype).at[ib.reshape(-1)].add(yb.reshape(-1, D)))(idx, y)


def hybrid_layer(x, ctx, mod_lat, mod_ctx, p, tabs, need_ctx):
    sh1, sc1, g1, sh2, sc2, g2 = jnp.split(mod_lat, 6, axis=-1)
    csh1, csc1, cg1, csh2, csc2, cg2 = jnp.split(mod_ctx, 6, axis=-1)
    lat = project_side(modulate(rmsnorm(x, p["norm1_g"]), sh1, sc1), p, tabs, True)
    cx = project_side(modulate(rmsnorm(ctx, p["norm1_g"]), csh1, csc1), p, None, need_ctx)
    k_mla = jnp.concatenate([cx["k_mla"], lat["k_mla"]], axis=1)
    v_mla = jnp.concatenate([cx["v_mla"], lat["v_mla"]], axis=1)
    k_gqa = jnp.concatenate([cx["k_gqa"], lat["k_gqa"]], axis=1)
    v_gqa = jnp.concatenate([cx["v_gqa"], lat["v_gqa"]], axis=1)
    x = x + g1 * mix_side(lat, k_mla, v_mla, k_gqa, v_gqa, p)
    x = x + g2 * expert_choice_ffn(modulate(rmsnorm(x, p["norm2_g"]), sh2, sc2), p)
    if need_ctx:
        ctx = ctx + cg1 * mix_side(cx, cx["k_mla"], cx["v_mla"], cx["k_gqa"], cx["v_gqa"], p)
        ctx = ctx + cg2 * expert_choice_ffn(modulate(rmsnorm(ctx, p["norm2_g"]), csh2, csc2), p)
    return x, ctx


def setup_inputs(seed: int = 0) -> dict:
    key = jax.random.key(seed)
    ks = iter(jax.random.split(key, 32))

    def nrm(shape, scale):
        return jax.random.normal(next(ks), shape, jnp.float32) * scale

    def gain(shape):
        return 1.0 + nrm(shape, 0.05)

    L = DEPTH
    return {
        "x": nrm((BATCH, SEQ, D_MODEL), 1.0),
        "c": nrm((BATCH, D_MODEL), 1.0),
        "ctx": nrm((BATCH, CTX_LEN, D_MODEL), 1.0),
        "c_ctx": nrm((D_MODEL,), 1.0),
        "mod_w": nrm((L, D_MODEL, 6 * D_MODEL), 0.5 * D_MODEL ** -0.5),
        "mod_b": nrm((L, 6 * D_MODEL), 0.02),
        "norm1_g": gain((L, D_MODEL)),
        "w_in": nrm((L, D_MODEL, IN_WIDTH), D_MODEL ** -0.5),
        "mla_q_norm_g": gain((L, MLA_Q_RANK)),
        "mla_w_uq": nrm((L, MLA_Q_RANK, MLA_HEADS * (MLA_NOPE + MLA_ROPE)), MLA_Q_RANK ** -0.5),
        "mla_kv_norm_g": gain((L, MLA_KV_RANK)),
        "mla_w_ukv": nrm((L, MLA_KV_RANK, MLA_HEADS * (MLA_NOPE + MLA_V)), MLA_KV_RANK ** -0.5),
        "gqa_q_norm_g": gain((L, GQA_HEAD_DIM)),
        "gqa_k_norm_g": gain((L, GQA_HEAD_DIM)),
        "conv_dw_w": nrm((L, CONV_WIDTH, CONV_CH), CONV_WIDTH ** -0.5),
        "conv_dw_b": nrm((L, CONV_CH), 0.02),
        "conv_ln_g": gain((L, CONV_CH)),
        "conv_ln_b": nrm((L, CONV_CH), 0.02),
        "mix_out_g": gain((L, MIX_WIDTH)),
        "w_out": nrm((L, MIX_WIDTH, D_MODEL), MIX_WIDTH ** -0.5),
        "norm2_g": gain((L, D_MODEL)),
        "router_w": nrm((L, D_MODEL, N_EXPERTS), D_MODEL ** -0.5),
        "exp_w1": nrm((L, N_EXPERTS, D_MODEL, EXPERT_FF), D_MODEL ** -0.5),
        "exp_w3": nrm((L, N_EXPERTS, D_MODEL, EXPERT_FF), D_MODEL ** -0.5),
        "exp_w2": nrm((L, N_EXPERTS, EXPERT_FF, D_MODEL), EXPERT_FF ** -0.5),
        "final_g": gain((D_MODEL,)),
    }


def reference(x, c, ctx, c_ctx, mod_w, mod_b, norm1_g, w_in, mla_q_norm_g, mla_w_uq, mla_kv_norm_g,
              mla_w_ukv, gqa_q_norm_g, gqa_k_norm_g, conv_dw_w, conv_dw_b, conv_ln_g, conv_ln_b,
              mix_out_g, w_out, norm2_g, router_w, exp_w1, exp_w3, exp_w2, final_g):
    n = x.shape[1]
    tabs = axial_rope_tables(n, MLA_ROPE) + axial_rope_tables(n, GQA_HEAD_DIM)
    for l in range(DEPTH):
        p = {
            "norm1_g": norm1_g[l], "w_in": w_in[l],
            "mla_q_norm_g": mla_q_norm_g[l], "mla_w_uq": mla_w_uq[l],
            "mla_kv_norm_g": mla_kv_norm_g[l], "mla_w_ukv": mla_w_ukv[l],
            "gqa_q_norm_g": gqa_q_norm_g[l], "gqa_k_norm_g": gqa_k_norm_g[l],
            "conv_dw_w": conv_dw_w[l], "conv_dw_b": conv_dw_b[l],
            "conv_ln_g": conv_ln_g[l], "conv_ln_b": conv_ln_b[l],
            "mix_out_g": mix_out_g[l], "w_out": w_out[l], "norm2_g": norm2_g[l],
            "router_w": router_w[l], "exp_w1": exp_w1[l], "exp_w3": exp_w3[l], "exp_w2": exp_w2[l],
        }
        mod_lat = (jax.nn.silu(c) @ mod_w[l] + mod_b[l])[:, None, :]
        mod_ctx = jax.nn.silu(c_ctx) @ mod_w[l] + mod_b[l]
        x, ctx = hybrid_layer(x, ctx, mod_lat, mod_ctx, p, tabs, l < DEPTH - 1)
    return rmsnorm(x, final_g)
```

```python
import functools
import math

import jax
import jax.numpy as jnp
from jax import lax
from jax.experimental import pallas as pl
from jax.experimental.pallas import tpu as pltpu

F32 = jnp.float32
BF16 = jnp.bfloat16
HIGHEST = lax.Precision.HIGHEST

GRID_W = 64
ROPE_BASE = 10000.0
EPS = 1e-6

MLA_HEADS = 8
MLA_Q_RANK = 256
MLA_KV_RANK = 128
MLA_NOPE = 64
MLA_ROPE = 32
MLA_V = 64
GQA_HEADS = 4
GQA_KV_HEADS = 2
GQA_HEAD_DIM = 64
CONV_CH = 256
CONV_WIDTH = 31
N_EXPERTS = 16
CAPACITY_FACTOR = 2

MLA_SCALE = 1.0 / math.sqrt(MLA_NOPE + MLA_ROPE)
GQA_SCALE = 1.0 / math.sqrt(GQA_HEAD_DIM)

LANES = 128
N_QHEADS = MLA_HEADS + GQA_HEADS
N_KHEADS = MLA_HEADS + GQA_KV_HEADS
N_PAIRS = N_QHEADS // 2
TM = 256
TQ = 256
TK = 256
HALO = 16
VMEM_LIMIT = 56 * 1024 * 1024
MOD_ROWS = 8
CTX_ROW = MOD_ROWS - 1

C_CQ = 0
C_CKV = C_CQ + MLA_Q_RANK
C_ZQ = C_CKV + MLA_KV_RANK
C_ZK = C_ZQ + GQA_HEADS * LANES
C_ZV = C_ZK + GQA_KV_HEADS * LANES
C_ZA = C_ZV + GQA_HEADS * LANES
C_ZB = C_ZA + CONV_CH
C_KR = C_ZB + CONV_CH
IN_PAD = C_KR + LANES


def _cparams(sem):
    return pltpu.CompilerParams(dimension_semantics=sem, vmem_limit_bytes=VMEM_LIMIT)


def _rms(x, g):
    return x * lax.rsqrt(jnp.mean(x * x, axis=-1, keepdims=True) + EPS) * g


def _head_rms(x, g, width):
    return x * lax.rsqrt(jnp.sum(x * x, axis=-1, keepdims=True) * (1.0 / width) + EPS) * g


def _rope(x, cos, sin, lo, half):
    lane = lax.broadcasted_iota(jnp.int32, x.shape, 1)
    first = (lane >= lo) & (lane < lo + half)
    rot = jnp.where(first, pltpu.roll(x, LANES - half, 1), pltpu.roll(x, half, 1))
    return x * cos + rot * sin


def _mod_kernel(c_ref, w_ref, b_ref, o_ref):
    c = c_ref[...]
    a = c * jax.nn.sigmoid(c)
    o_ref[...] = jnp.dot(a, w_ref[...], preferred_element_type=F32, precision=HIGHEST) + b_ref[...]


def _modulation(cvec, mod_w, mod_b):
    L, D, W = mod_w.shape
    tn = 1536
    return pl.pallas_call(
        _mod_kernel,
        grid=(L, W // tn),
        in_specs=[pl.BlockSpec((MOD_ROWS, D), lambda l, n: (0, 0)),
                  pl.BlockSpec((None, D, tn), lambda l, n: (l, 0, n)),
                  pl.BlockSpec((None, 1, tn), lambda l, n: (l, 0, n))],
        out_specs=pl.BlockSpec((None, MOD_ROWS, tn), lambda l, n: (l, 0, n)),
        out_shape=jax.ShapeDtypeStruct((L, MOD_ROWS, W), F32),
        compiler_params=_cparams(("arbitrary", "arbitrary")),
        name="modulation",
    )(cvec, mod_w, mod_b.reshape(L, 1, W))


def _proj_kernel(*refs, has_moe):
    if has_moe:
        x_ref, moe_ref, modp_ref, refs = refs[0], refs[1], refs[2], refs[3:]
    else:
        x_ref, refs = refs[0], refs[1:]
    (mod_ref, n1_ref, win_ref, gq_ref, wuq_ref, gkv_ref, wukv_ref, ggq_ref, ggk_ref,
     cm_ref, sm_ref, cg_ref, sg_ref) = refs[:13]
    outs = refs[13:]
    if has_moe:
        xo_ref, outs = outs[0], outs[1:]
    q_ref, k_ref, v_ref, u_ref = outs

    x = x_ref[...]
    if has_moe:
        x = x + modp_ref[5:6, :] * moe_ref[...]
        xo_ref[...] = x
    hm = _rms(x, n1_ref[...]) * (1.0 + mod_ref[1:2, :]) + mod_ref[0:1, :]
    z = jnp.dot(hm.astype(BF16), win_ref[...], preferred_element_type=F32)

    cos_m, sin_m, cos_g, sin_g = cm_ref[...], sm_ref[...], cg_ref[...], sg_ref[...]

    cqn = _rms(z[:, C_CQ:C_CQ + MLA_Q_RANK], gq_ref[...])
    q = jnp.dot(cqn.astype(BF16), wuq_ref[...], preferred_element_type=F32)
    for h in range(MLA_HEADS):
        q_ref[h] = _rope(q[:, h * LANES:(h + 1) * LANES], cos_m, sin_m, MLA_NOPE, MLA_ROPE // 2).astype(BF16)

    ckvn = _rms(z[:, C_CKV:C_CKV + MLA_KV_RANK], gkv_ref[...])
    kv = jnp.dot(ckvn.astype(BF16), wukv_ref[...], preferred_element_type=F32)
    kr = _rope(z[:, C_KR:C_KR + LANES], cos_m, sin_m, MLA_NOPE, MLA_ROPE // 2)
    for h in range(MLA_HEADS):
        k_ref[h] = (kv[:, h * LANES:(h + 1) * LANES] + kr).astype(BF16)
        v_ref[h] = kv[:, (MLA_HEADS + h) * LANES:(MLA_HEADS + h + 1) * LANES].astype(BF16)

    for h in range(GQA_HEADS):
        zq = z[:, C_ZQ + h * LANES:C_ZQ + (h + 1) * LANES]
        q_ref[MLA_HEADS + h] = _rope(_head_rms(zq, ggq_ref[...], GQA_HEAD_DIM), cos_g, sin_g,
                                     0, GQA_HEAD_DIM // 2).astype(BF16)
        v_ref[MLA_HEADS + h] = z[:, C_ZV + h * LANES:C_ZV + (h + 1) * LANES].astype(BF16)
    for h in range(GQA_KV_HEADS):
        zk = z[:, C_ZK + h * LANES:C_ZK + (h + 1) * LANES]
        k_ref[MLA_HEADS + h] = _rope(_head_rms(zk, ggk_ref[...], GQA_HEAD_DIM), cos_g, sin_g,
                                     0, GQA_HEAD_DIM // 2).astype(BF16)

    u_ref[...] = z[:, C_ZA:C_ZA + CONV_CH] * jax.nn.sigmoid(z[:, C_ZB:C_ZB + CONV_CH])


def _mod_row(b, j):
    return jnp.where(j == 0, CTX_ROW, b)


def _project(xs, moe, modp, mod, lw, tabs):
    B, S, D = xs.shape
    nt = S // TM
    has_moe = moe is not None
    tok = lambda b, j: (b, j, 0)
    const2 = lambda b, j: (0, 0)
    modspec = pl.BlockSpec((None, 6, D), lambda b, j: (_mod_row(b, j), 0, 0))
    in_specs = [pl.BlockSpec((None, TM, D), tok)]
    args = [xs]
    if has_moe:
        in_specs += [pl.BlockSpec((None, TM, D), tok), modspec]
        args += [moe, modp]
    weights = [lw["norm1_g"], lw["w_in"], lw["mla_q_norm_g"], lw["w_uq"], lw["mla_kv_norm_g"], lw["w_ukv"],
               lw["gqa_q_norm_g"], lw["gqa_k_norm_g"]]
    in_specs += [modspec] + [pl.BlockSpec(w.shape, const2) for w in weights]
    in_specs += [pl.BlockSpec((TM, LANES), lambda b, j: (j, 0))] * 4
    args += [mod] + weights + list(tabs)
    head_spec = lambda n: pl.BlockSpec((None, n, TM, LANES), lambda b, j: (b, 0, j, 0))
    out_specs = [head_spec(N_QHEADS), head_spec(N_KHEADS), head_spec(N_QHEADS),
                 pl.BlockSpec((None, TM, CONV_CH), tok)]
    out_shape = [jax.ShapeDtypeStruct((B, N_QHEADS, S, LANES), BF16),
                 jax.ShapeDtypeStruct((B, N_KHEADS, S, LANES), BF16),
                 jax.ShapeDtypeStruct((B, N_QHEADS, S, LANES), BF16),
                 jax.ShapeDtypeStruct((B, S, CONV_CH), F32)]
    if has_moe:
        out_specs = [pl.BlockSpec((None, TM, D), tok)] + out_specs
        out_shape = [jax.ShapeDtypeStruct((B, S, D), F32)] + out_shape
    outs = pl.pallas_call(
        functools.partial(_proj_kernel, has_moe=has_moe),
        grid=(B, nt), in_specs=in_specs, out_specs=out_specs, out_shape=out_shape,
        compiler_params=_cparams(("parallel", "parallel")),
        name="project",
    )(*args)
    if has_moe:
        return outs[0], outs[1:]
    return xs, outs


def _attn_kernel(q_ref, k0_ref, k1_ref, v_ref, o_ref, *, tile_off, ctx_chunks, all_chunks):
    hp = pl.program_id(1)
    j = pl.program_id(2) + tile_off
    scale = jnp.where(hp < MLA_HEADS // 2, MLA_SCALE, GQA_SCALE).astype(F32)
    nchunks = jnp.where(j == 0, ctx_chunks, all_chunks)
    out = None
    for i, k_ref in enumerate((k0_ref, k1_ref)):
        q = q_ref[i]

        def body(c, carry, i=i, q=q, k_ref=k_ref):
            m, l, acc = carry
            r0 = pl.multiple_of(c * TK, TK)
            s = lax.dot_general(q, k_ref[pl.ds(r0, TK), :], (((1,), (1,)), ((), ())),
                                preferred_element_type=F32) * scale
            m_new = jnp.maximum(m, jnp.max(s, axis=-1, keepdims=True))
            alpha = jnp.exp(m - m_new)
            p = jnp.exp(s - m_new)
            l = alpha * l + jnp.sum(p, axis=-1, keepdims=True)
            acc = alpha * acc + jnp.dot(p.astype(BF16), v_ref[i, pl.ds(r0, TK), :], preferred_element_type=F32)
            return m_new, l, acc

        init = (jnp.full((TQ, 1), -jnp.inf, F32), jnp.zeros((TQ, 1), F32), jnp.zeros((TQ, LANES), F32))
        _, l, acc = lax.fori_loop(0, nchunks, body, init)
        o = acc / l
        out = o if out is None else out + o
    o_ref[...] = out


def _kv_head(h):
    group = GQA_HEADS // GQA_KV_HEADS
    return jnp.where(h < MLA_HEADS, h, MLA_HEADS + (h - MLA_HEADS) // group)


def _attention(q, k, v, n_ctx, tile_off):
    B, _, S, _ = q.shape
    nq = S // TQ - tile_off
    kspec = lambda i: pl.BlockSpec((None, None, S, LANES), lambda b, hp, j: (b, _kv_head(2 * hp + i), 0, 0))
    return pl.pallas_call(
        functools.partial(_attn_kernel, tile_off=tile_off, ctx_chunks=n_ctx // TK, all_chunks=S // TK),
        grid=(B, N_PAIRS, nq),
        in_specs=[pl.BlockSpec((None, 2, TQ, LANES), lambda b, hp, j: (b, hp, j + tile_off, 0)),
                  kspec(0), kspec(1),
                  pl.BlockSpec((None, 2, S, LANES), lambda b, hp, j: (b, hp, 0, 0))],
        out_specs=pl.BlockSpec((None, TQ, LANES), lambda b, hp, j: (b, j, hp)),
        out_shape=jax.ShapeDtypeStruct((B, nq * TQ, N_PAIRS * LANES), F32),
        compiler_params=_cparams(("parallel", "parallel", "arbitrary")),
        name="attention",
    )(q, k, k, v)


def _mix_kernel(o_ref, up_ref, uc_ref, un_ref, x_ref, mod_ref, cw_ref, cb_ref, lg_ref, lb_ref,
                mg_ref, wout_ref, n2_ref, rw_ref,
                xo_ref, h_ref, aff_ref, ext_ref, *, tile_off, last_tile):
    j = pl.program_id(1) + tile_off
    has_prev = j >= 2
    has_next = (j >= 1) & (j < last_tile)
    ext_ref[0:HALO, :] = jnp.where(has_prev, up_ref[TM - HALO:TM, :], 0.0)
    ext_ref[HALO:HALO + TM, :] = uc_ref[...]
    ext_ref[HALO + TM:HALO + TM + HALO, :] = jnp.where(has_next, un_ref[0:HALO, :], 0.0)
    y = jnp.zeros((TM, CONV_CH), F32) + cb_ref[...]
    for t in range(CONV_WIDTH):
        r = HALO - CONV_WIDTH // 2 + t
        y = y + ext_ref[r:r + TM, :] * cw_ref[t:t + 1, :]
    mu = jnp.mean(y, axis=-1, keepdims=True)
    yc = y - mu
    yn = yc * lax.rsqrt(jnp.mean(yc * yc, axis=-1, keepdims=True) + EPS) * lg_ref[...] + lb_ref[...]
    o_conv = yn * jax.nn.sigmoid(yn)

    o = o_ref[...]
    w_mla = MLA_HEADS * MLA_V
    w_att = w_mla + GQA_HEADS * GQA_HEAD_DIM
    cat = jnp.concatenate([
        _rms(o[:, :w_mla], mg_ref[:, :w_mla]).astype(BF16),
        _rms(o[:, w_mla:w_att], mg_ref[:, w_mla:w_att]).astype(BF16),
        _rms(o_conv, mg_ref[:, w_att:]).astype(BF16)], axis=-1)
    mix = jnp.dot(cat, wout_ref[...], preferred_element_type=F32)
    x = x_ref[...] + mod_ref[2:3, :] * mix
    xo_ref[...] = x
    hm = _rms(x, n2_ref[...]) * (1.0 + mod_ref[4:5, :]) + mod_ref[3:4, :]
    h_ref[...] = hm.astype(BF16)
    logits = lax.dot_general(rw_ref[...], hm, (((1,), (1,)), ((), ())),
                             preferred_element_type=F32, precision=HIGHEST)
    e = jnp.exp(logits - jnp.max(logits, axis=0, keepdims=True))
    aff_ref[...] = e / jnp.sum(e, axis=0, keepdims=True)


def _mix(o, u, xs, mod, lw, tile_off):
    B, S_o, _ = o.shape
    D = xs.shape[-1]
    nt_all = u.shape[1] // TM
    nt = S_o // TM
    const2 = lambda b, j: (0, 0)
    weights = [lw["conv_w"], lw["conv_dw_b"], lw["conv_ln_g"], lw["conv_ln_b"], lw["mix_out_g"], lw["w_out"],
               lw["norm2_g"], lw["router_wt"]]
    uspec = lambda d: pl.BlockSpec(
        (None, TM, CONV_CH), lambda b, j: (b, jnp.clip(j + tile_off + d, 0, nt_all - 1), 0))
    return pl.pallas_call(
        functools.partial(_mix_kernel, tile_off=tile_off, last_tile=nt_all - 1),
        grid=(B, nt),
        in_specs=[pl.BlockSpec((None, TM, o.shape[-1]), lambda b, j: (b, j, 0)),
                  uspec(-1), uspec(0), uspec(1),
                  pl.BlockSpec((None, TM, D), lambda b, j: (b, j + tile_off, 0)),
                  pl.BlockSpec((None, 6, D), lambda b, j: (_mod_row(b, j + tile_off), 0, 0))]
                 + [pl.BlockSpec(w.shape, const2) for w in weights],
        out_specs=[pl.BlockSpec((None, TM, D), lambda b, j: (b, j, 0)),
                   pl.BlockSpec((None, TM, D), lambda b, j: (b, j, 0)),
                   pl.BlockSpec((None, N_EXPERTS, TM), lambda b, j: (b, 0, j))],
        out_shape=[jax.ShapeDtypeStruct((B, S_o, D), F32),
                   jax.ShapeDtypeStruct((B, S_o, D), BF16),
                   jax.ShapeDtypeStruct((B, N_EXPERTS, S_o), F32)],
        scratch_shapes=[pltpu.VMEM((TM + 2 * HALO, CONV_CH), F32)],
        compiler_params=_cparams(("parallel", "parallel")),
        name="mix",
    )(o, u, u, u, xs, mod, *weights)


def _moe_kernel(x_ref, g_ref, w1_ref, w3_ref, w2_ref, y_ref, w1b, w3b, w2b, *, nchunk, rows):
    f = pl.program_id(1)
    w1b[...] = w1_ref[...].astype(BF16)
    w3b[...] = w3_ref[...].astype(BF16)
    w2b[...] = w2_ref[...].astype(BF16)

    def body(c, carry):
        r0 = pl.multiple_of(c * rows, rows)
        xb = x_ref[pl.ds(r0, rows), :]
        h1 = jnp.dot(xb, w1b[...], preferred_element_type=F32)
        h3 = jnp.dot(xb, w3b[...], preferred_element_type=F32)
        hid = (h1 * jax.nn.sigmoid(h1) * h3).astype(BF16)
        part = jnp.dot(hid, w2b[...], preferred_element_type=F32)

        @pl.when(f == 0)
        def _():
            y_ref[pl.ds(r0, rows), :] = part

        @pl.when(f > 0)
        def _():
            y_ref[pl.ds(r0, rows), :] += part
        return carry

    lax.fori_loop(0, nchunk, body, 0)

    @pl.when(f == pl.num_programs(1) - 1)
    def _():
        y_ref[...] = y_ref[...] * g_ref[...]


def _moe_tiles(R, FF):
    rows = next(r for r in range(512, 15, -16) if R % r == 0)
    tf = next(t for t in range(512, LANES - 1, -LANES) if FF % t == 0)
    return rows, tf


def _experts(xg, gate, w1, w3, w2):
    E, R, D = xg.shape
    FF = w1.shape[-1]
    rows, tf = _moe_tiles(R, FF)
    return pl.pallas_call(
        functools.partial(_moe_kernel, nchunk=R // rows, rows=rows),
        grid=(E, FF // tf),
        in_specs=[pl.BlockSpec((None, R, D), lambda e, f: (e, 0, 0)),
                  pl.BlockSpec((None, R, 1), lambda e, f: (e, 0, 0)),
                  pl.BlockSpec((None, D, tf), lambda e, f: (e, 0, f)),
                  pl.BlockSpec((None, D, tf), lambda e, f: (e, 0, f)),
                  pl.BlockSpec((None, tf, D), lambda e, f: (e, f, 0))],
        out_specs=pl.BlockSpec((None, R, D), lambda e, f: (e, 0, 0)),
        out_shape=jax.ShapeDtypeStruct((E, R, D), F32),
        scratch_shapes=[pltpu.VMEM((D, tf), BF16), pltpu.VMEM((D, tf), BF16), pltpu.VMEM((tf, D), BF16)],
        compiler_params=_cparams(("parallel", "arbitrary")),
        name="experts",
    )(xg, gate, w1, w3, w2)


def _final_kernel(x_ref, moe_ref, mod_ref, g_ref, o_ref):
    x = x_ref[...] + mod_ref[5:6, :] * moe_ref[...]
    o_ref[...] = _rms(x, g_ref[...])


def _final(x, moe, mod, g):
    B, N, D = x.shape
    tok = lambda b, j: (b, j, 0)
    return pl.pallas_call(
        _final_kernel,
        grid=(B, N // TM),
        in_specs=[pl.BlockSpec((None, TM, D), tok), pl.BlockSpec((None, TM, D), tok),
                  pl.BlockSpec((None, 6, D), lambda b, j: (b, 0, 0)),
                  pl.BlockSpec((1, D), lambda b, j: (0, 0))],
        out_specs=pl.BlockSpec((None, TM, D), tok),
        out_shape=jax.ShapeDtypeStruct((B, N, D), F32),
        compiler_params=_cparams(("parallel", "parallel")),
        name="final_norm",
    )(x, moe, mod, g)


def _pad_cols(w, lo, width=LANES):
    return jnp.pad(w, ((0, 0), (lo, width - lo - w.shape[1])))


def _prep_layer(p):
    w_in = p["w_in"]
    o = 0
    parts = {}
    for name, size in (("cq", MLA_Q_RANK), ("ckv", MLA_KV_RANK), ("kr", MLA_ROPE),
                       ("zq", GQA_HEADS * GQA_HEAD_DIM), ("zk", GQA_KV_HEADS * GQA_HEAD_DIM),
                       ("zv", GQA_KV_HEADS * GQA_HEAD_DIM), ("zc", 2 * CONV_CH)):
        parts[name] = w_in[:, o:o + size]
        o += size
    hd = GQA_HEAD_DIM
    group = GQA_HEADS // GQA_KV_HEADS
    cols = [parts["cq"], parts["ckv"]]
    cols += [_pad_cols(parts["zq"][:, h * hd:(h + 1) * hd], 0) for h in range(GQA_HEADS)]
    cols += [_pad_cols(parts["zk"][:, h * hd:(h + 1) * hd], 0) for h in range(GQA_KV_HEADS)]
    cols += [_pad_cols(parts["zv"][:, (h // group) * hd:(h // group + 1) * hd], (h % 2) * hd)
             for h in range(GQA_HEADS)]
    cols += [parts["zc"], _pad_cols(parts["kr"], MLA_NOPE)]
    w_in_p = jnp.concatenate(cols, axis=1).astype(BF16)

    dq = MLA_NOPE + MLA_ROPE
    w_uq = jnp.concatenate([_pad_cols(p["mla_w_uq"][:, h * dq:(h + 1) * dq], 0) for h in range(MLA_HEADS)],
                           axis=1).astype(BF16)
    dkv = MLA_NOPE + MLA_V
    wk = [_pad_cols(p["mla_w_ukv"][:, h * dkv:h * dkv + MLA_NOPE], 0) for h in range(MLA_HEADS)]
    wv = [_pad_cols(p["mla_w_ukv"][:, h * dkv + MLA_NOPE:(h + 1) * dkv], (h % 2) * MLA_V) for h in range(MLA_HEADS)]
    w_ukv = jnp.concatenate(wk + wv, axis=1).astype(BF16)

    row = lambda a: a.reshape(1, -1)
    return {
        "norm1_g": row(p["norm1_g"]), "w_in": w_in_p,
        "mla_q_norm_g": row(p["mla_q_norm_g"]), "w_uq": w_uq,
        "mla_kv_norm_g": row(p["mla_kv_norm_g"]), "w_ukv": w_ukv,
        "gqa_q_norm_g": _pad_cols(row(p["gqa_q_norm_g"]), 0), "gqa_k_norm_g": _pad_cols(row(p["gqa_k_norm_g"]), 0),
        "conv_w": jnp.pad(p["conv_dw_w"], ((0, 32 - CONV_WIDTH), (0, 0))),
        "conv_dw_b": row(p["conv_dw_b"]), "conv_ln_g": row(p["conv_ln_g"]), "conv_ln_b": row(p["conv_ln_b"]),
        "mix_out_g": row(p["mix_out_g"]), "w_out": p["w_out"].astype(BF16),
        "norm2_g": row(p["norm2_g"]), "router_wt": p["router_w"].T,
    }


def _rope_tables(n, n_ctx):
    rows = n // GRID_W
    row = jnp.repeat(jnp.arange(rows), GRID_W).astype(F32)
    col = jnp.tile(jnp.arange(GRID_W), rows).astype(F32)
    tabs = []
    for d_rot, lo in ((MLA_ROPE, MLA_NOPE), (GQA_HEAD_DIM, 0)):
        d_axis = d_rot // 2
        inv = ROPE_BASE ** (-jnp.arange(0, d_axis, 2, dtype=F32) / d_axis)
        ang = jnp.concatenate([row[:, None] * inv, col[:, None] * inv], axis=-1)
        cos, sin = jnp.cos(ang), jnp.sin(ang)
        cos_b = jnp.pad(jnp.concatenate([cos, cos], axis=1) - 1.0, ((n_ctx, 0), (lo, LANES - lo - d_rot))) + 1.0
        sin_b = jnp.pad(jnp.concatenate([-sin, sin], axis=1), ((n_ctx, 0), (lo, LANES - lo - d_rot)))
        tabs += [cos_b, sin_b]
    return tabs


def _route(aff, stride, cap, row0):
    gate, idx = lax.top_k(aff, cap)
    B = aff.shape[0]
    gidx = idx + (row0 + jnp.arange(B, dtype=idx.dtype) * stride)[:, None, None]
    E = aff.shape[1]
    return gidx.transpose(1, 0, 2).reshape(E, B * cap), gate.transpose(1, 0, 2).reshape(E, B * cap)


def _moe_layer(aff, h, n_ctx, with_ctx, w1, w3, w2):
    B, S_o, D = h.shape
    lat0 = n_ctx if with_ctx else 0
    n = S_o - lat0
    sets = [_route(aff[:, :, lat0:], S_o, CAPACITY_FACTOR * n // N_EXPERTS, lat0)]
    if with_ctx:
        sets.append(_route(aff[:, :, :n_ctx], S_o, CAPACITY_FACTOR * n_ctx // N_EXPERTS, 0))
    gidx = jnp.concatenate([s[0] for s in sets], axis=1)
    gate = jnp.concatenate([s[1] for s in sets], axis=1)
    h_flat = h.reshape(B * S_o, D)
    xg = h_flat[gidx]
    y = _experts(xg, gate[..., None], w1, w3, w2)
    out = jnp.zeros((B * S_o, D), F32).at[gidx.reshape(-1)].add(y.reshape(-1, D))
    return out.reshape(B, S_o, D)


def kernel(x, c, ctx, c_ctx, mod_w, mod_b, norm1_g, w_in, mla_q_norm_g, mla_w_uq, mla_kv_norm_g, mla_w_ukv,
           gqa_q_norm_g, gqa_k_norm_g, conv_dw_w, conv_dw_b, conv_ln_g, conv_ln_b, mix_out_g, w_out, norm2_g,
           router_w, exp_w1, exp_w3, exp_w2, final_g):
    B, N, D = x.shape
    n_ctx = ctx.shape[1]
    depth = mod_w.shape[0]
    assert n_ctx == TM == TQ and N % TM == 0 and n_ctx % TK == 0
    stacked = {"norm1_g": norm1_g, "w_in": w_in, "mla_q_norm_g": mla_q_norm_g, "mla_w_uq": mla_w_uq,
               "mla_kv_norm_g": mla_kv_norm_g, "mla_w_ukv": mla_w_ukv, "gqa_q_norm_g": gqa_q_norm_g,
               "gqa_k_norm_g": gqa_k_norm_g, "conv_dw_w": conv_dw_w, "conv_dw_b": conv_dw_b,
               "conv_ln_g": conv_ln_g, "conv_ln_b": conv_ln_b, "mix_out_g": mix_out_g, "w_out": w_out,
               "norm2_g": norm2_g, "router_w": router_w}

    assert B <= CTX_ROW
    cvec = jnp.zeros((MOD_ROWS, D), F32).at[:B].set(c).at[CTX_ROW].set(c_ctx)
    mods = _modulation(cvec, mod_w, mod_b).reshape(depth, MOD_ROWS, 6, D)
    tabs = _rope_tables(N, n_ctx)

    xs = jnp.concatenate([ctx, x], axis=1)
    moe, modp = None, None
    for l in range(depth):
        lw = _prep_layer({k: v[l] for k, v in stacked.items()})
        last = l == depth - 1
        tile_off = 1 if last else 0
        xs, (q, k, v, u) = _project(xs, moe, modp, mods[l], lw, tabs)
        o = _attention(q, k, v, n_ctx, tile_off)
        xs, h, aff = _mix(o, u, xs, mods[l], lw, tile_off)
        moe = _moe_layer(aff, h, n_ctx, not last, exp_w1[l], exp_w3[l], exp_w2[l])
        modp = mods[l]
    return _final(xs, moe, modp, final_g.reshape(1, D))
```

```python
import functools
import math

import jax
import jax.numpy as jnp
from jax import lax
from jax.experimental import pallas as pl
from jax.experimental.pallas import tpu as pltpu

F32 = jnp.float32
BF16 = jnp.bfloat16
HIGHEST = lax.Precision.HIGHEST

GRID_W = 64
ROPE_BASE = 10000.0
EPS = 1e-6

MLA_HEADS = 8
MLA_Q_RANK = 256
MLA_KV_RANK = 128
MLA_NOPE = 64
MLA_ROPE = 32
MLA_V = 64
GQA_HEADS = 4
GQA_KV_HEADS = 2
GQA_HEAD_DIM = 64
CONV_CH = 256
CONV_WIDTH = 31
N_EXPERTS = 16
CAPACITY_FACTOR = 2

MLA_SCALE = 1.0 / math.sqrt(MLA_NOPE + MLA_ROPE)
GQA_SCALE = 1.0 / math.sqrt(GQA_HEAD_DIM)

LANES = 128
N_QHEADS = MLA_HEADS + GQA_HEADS
N_KHEADS = MLA_HEADS + GQA_KV_HEADS
N_PAIRS = N_QHEADS // 2
TM = 256
TQ = 256
TK_MAX = 1024
ONES_LANE = (MLA_V, 0)
LOG2E = math.log2(math.e)
HALO = 16
VMEM_LIMIT = 56 * 1024 * 1024
MOD_ROWS = 8
CTX_ROW = MOD_ROWS - 1

C_CQ = 0
C_CKV = C_CQ + MLA_Q_RANK
C_ZQ = C_CKV + MLA_KV_RANK
C_ZK = C_ZQ + GQA_HEADS * LANES
C_ZV = C_ZK + GQA_KV_HEADS * LANES
C_ZA = C_ZV + GQA_HEADS * LANES
C_ZB = C_ZA + CONV_CH
C_KR = C_ZB + CONV_CH
IN_PAD = C_KR + LANES


def _cparams(sem):
    return pltpu.CompilerParams(dimension_semantics=sem, vmem_limit_bytes=VMEM_LIMIT)


def _rms(x, g):
    return x * lax.rsqrt(jnp.mean(x * x, axis=-1, keepdims=True) + EPS) * g


def _head_rms(x, g, width):
    return x * lax.rsqrt(jnp.sum(x * x, axis=-1, keepdims=True) * (1.0 / width) + EPS) * g


def _rope(x, cos, sin, lo, half):
    lane = lax.broadcasted_iota(jnp.int32, x.shape, 1)
    first = (lane >= lo) & (lane < lo + half)
    rot = jnp.where(first, pltpu.roll(x, LANES - half, 1), pltpu.roll(x, half, 1))
    return x * cos + rot * sin


def _ones_col(h):
    lane = lax.broadcasted_iota(jnp.int32, (1, LANES), 1)
    return (lane == ONES_LANE[h % 2]).astype(F32)


def _mod_kernel(c_ref, w_ref, b_ref, o_ref):
    c = c_ref[...]
    a = c * jax.nn.sigmoid(c)
    o_ref[...] = jnp.dot(a, w_ref[...], preferred_element_type=F32, precision=HIGHEST) + b_ref[...]


def _modulation(cvec, mod_w, mod_b):
    L, D, W = mod_w.shape
    tn = 1536
    return pl.pallas_call(
        _mod_kernel,
        grid=(L, W // tn),
        in_specs=[pl.BlockSpec((MOD_ROWS, D), lambda l, n: (0, 0)),
                  pl.BlockSpec((None, D, tn), lambda l, n: (l, 0, n)),
                  pl.BlockSpec((None, 1, tn), lambda l, n: (l, 0, n))],
        out_specs=pl.BlockSpec((None, MOD_ROWS, tn), lambda l, n: (l, 0, n)),
        out_shape=jax.ShapeDtypeStruct((L, MOD_ROWS, W), F32),
        compiler_params=_cparams(("arbitrary", "arbitrary")),
        name="modulation",
    )(cvec, mod_w, mod_b.reshape(L, 1, W))


def _proj_kernel(*refs, has_moe):
    if has_moe:
        x_ref, moe_ref, modp_ref, refs = refs[0], refs[1], refs[2], refs[3:]
    else:
        x_ref, refs = refs[0], refs[1:]
    (mod_ref, n1_ref, win_ref, gq_ref, wuq_ref, gkv_ref, wukv_ref, ggq_ref, ggk_ref,
     cm_ref, sm_ref, cg_ref, sg_ref) = refs[:13]
    outs = refs[13:]
    if has_moe:
        xo_ref, outs = outs[0], outs[1:]
    q_ref, k_ref, v_ref, u_ref = outs

    x = x_ref[...]
    if has_moe:
        x = x + modp_ref[5:6, :] * moe_ref[...]
        xo_ref[...] = x
    hm = _rms(x, n1_ref[...]) * (1.0 + mod_ref[1:2, :]) + mod_ref[0:1, :]
    z = jnp.dot(hm.astype(BF16), win_ref[...], preferred_element_type=F32)

    cos_m, sin_m, cos_g, sin_g = cm_ref[...], sm_ref[...], cg_ref[...], sg_ref[...]

    cqn = _rms(z[:, C_CQ:C_CQ + MLA_Q_RANK], gq_ref[...])
    q = jnp.dot(cqn.astype(BF16), wuq_ref[...], preferred_element_type=F32)
    for h in range(MLA_HEADS):
        q_ref[h] = _rope(q[:, h * LANES:(h + 1) * LANES], cos_m, sin_m, MLA_NOPE, MLA_ROPE // 2).astype(BF16)

    ckvn = _rms(z[:, C_CKV:C_CKV + MLA_KV_RANK], gkv_ref[...])
    kv = jnp.dot(ckvn.astype(BF16), wukv_ref[...], preferred_element_type=F32)
    kr = _rope(z[:, C_KR:C_KR + LANES], cos_m, sin_m, MLA_NOPE, MLA_ROPE // 2)
    for h in range(MLA_HEADS):
        k_ref[h] = (kv[:, h * LANES:(h + 1) * LANES] + kr).astype(BF16)
        v_ref[h] = (kv[:, (MLA_HEADS + h) * LANES:(MLA_HEADS + h + 1) * LANES] + _ones_col(h)).astype(BF16)

    for h in range(GQA_HEADS):
        zq = z[:, C_ZQ + h * LANES:C_ZQ + (h + 1) * LANES]
        q_ref[MLA_HEADS + h] = _rope(_head_rms(zq, ggq_ref[...], GQA_HEAD_DIM), cos_g, sin_g,
                                     0, GQA_HEAD_DIM // 2).astype(BF16)
        v_ref[MLA_HEADS + h] = (z[:, C_ZV + h * LANES:C_ZV + (h + 1) * LANES] + _ones_col(h)).astype(BF16)
    for h in range(GQA_KV_HEADS):
        zk = z[:, C_ZK + h * LANES:C_ZK + (h + 1) * LANES]
        k_ref[MLA_HEADS + h] = _rope(_head_rms(zk, ggk_ref[...], GQA_HEAD_DIM), cos_g, sin_g,
                                     0, GQA_HEAD_DIM // 2).astype(BF16)

    u_ref[...] = z[:, C_ZA:C_ZA + CONV_CH] * jax.nn.sigmoid(z[:, C_ZB:C_ZB + CONV_CH])


def _mod_row(b, j):
    return jnp.where(j == 0, CTX_ROW, b)


def _project(xs, moe, modp, mod, lw, tabs):
    B, S, D = xs.shape
    nt = S // TM
    has_moe = moe is not None
    tok = lambda b, j: (b, j, 0)
    const2 = lambda b, j: (0, 0)
    modspec = pl.BlockSpec((None, 6, D), lambda b, j: (_mod_row(b, j), 0, 0))
    in_specs = [pl.BlockSpec((None, TM, D), tok)]
    args = [xs]
    if has_moe:
        in_specs += [pl.BlockSpec((None, TM, D), tok), modspec]
        args += [moe, modp]
    weights = [lw["norm1_g"], lw["w_in"], lw["mla_q_norm_g"], lw["w_uq"], lw["mla_kv_norm_g"], lw["w_ukv"],
               lw["gqa_q_norm_g"], lw["gqa_k_norm_g"]]
    in_specs += [modspec] + [pl.BlockSpec(w.shape, const2) for w in weights]
    in_specs += [pl.BlockSpec((TM, LANES), lambda b, j: (j, 0))] * 4
    args += [mod] + weights + list(tabs)
    head_spec = lambda n: pl.BlockSpec((None, n, TM, LANES), lambda b, j: (b, 0, j, 0))
    out_specs = [head_spec(N_QHEADS), head_spec(N_KHEADS), head_spec(N_QHEADS),
                 pl.BlockSpec((None, TM, CONV_CH), tok)]
    out_shape = [jax.ShapeDtypeStruct((B, N_QHEADS, S, LANES), BF16),
                 jax.ShapeDtypeStruct((B, N_KHEADS, S, LANES), BF16),
                 jax.ShapeDtypeStruct((B, N_QHEADS, S, LANES), BF16),
                 jax.ShapeDtypeStruct((B, S, CONV_CH), F32)]
    if has_moe:
        out_specs = [pl.BlockSpec((None, TM, D), tok)] + out_specs
        out_shape = [jax.ShapeDtypeStruct((B, S, D), F32)] + out_shape
    outs = pl.pallas_call(
        functools.partial(_proj_kernel, has_moe=has_moe),
        grid=(B, nt), in_specs=in_specs, out_specs=out_specs, out_shape=out_shape,
        compiler_params=_cparams(("parallel", "parallel")),
        name="project",
    )(*args)
    if has_moe:
        return outs[0], outs[1:]
    return xs, outs


def _scores(q, k):
    return lax.dot_general(q, k, (((1,), (1,)), ((), ())), preferred_element_type=F32)


def _softmax_update(s, v, m, acc, c):
    m_new = jnp.maximum(m, jnp.max(s, axis=-1, keepdims=True))
    alpha = jnp.exp2((m - m_new) * c)
    p = jnp.exp2((s - m_new) * c).astype(BF16)
    acc = alpha * acc + jnp.dot(p, v, preferred_element_type=F32)
    return m_new, acc


def _attn_kernel(q_ref, k0_ref, k1_ref, v_ref, o_ref, s_ref, *, tile_off, n_ctx, n_all, tk):
    hp = pl.program_id(1)
    j = pl.program_id(2) + tile_off
    c = jnp.where(hp < MLA_HEADS // 2, MLA_SCALE * LOG2E, GQA_SCALE * LOG2E).astype(F32)
    q0, q1 = q_ref[0], q_ref[1]
    init = (jnp.full((TQ, 1), -jnp.inf, F32), jnp.zeros((TQ, LANES), F32))

    def finish(acc0, acc1):
        lane = lax.broadcasted_iota(jnp.int32, (TQ, LANES), 1)
        l0 = acc0[:, ONES_LANE[0]:ONES_LANE[0] + 1]
        l1 = acc1[:, ONES_LANE[1]:ONES_LANE[1] + 1]
        o_ref[...] = jnp.where(lane < LANES // 2, acc0 / l0, acc1 / l1)

    @pl.when(j == 0)
    def _():
        _, acc0 = _softmax_update(_scores(q0, k0_ref[0:n_ctx, :]), v_ref[0, 0:n_ctx, :], *init, c)
        _, acc1 = _softmax_update(_scores(q1, k1_ref[0:n_ctx, :]), v_ref[1, 0:n_ctx, :], *init, c)
        finish(acc0, acc1)

    @pl.when(j > 0)
    def _():
        n = n_all // tk

        def put(slot, r):
            s_ref[slot, 0] = _scores(q0, k0_ref[pl.ds(r, tk), :])
            s_ref[slot, 1] = _scores(q1, k1_ref[pl.ds(r, tk), :])

        def step(slot, r, carry):
            m0, a0, m1, a1 = carry
            m0, a0 = _softmax_update(s_ref[slot, 0], v_ref[0, pl.ds(r, tk), :], m0, a0, c)
            m1, a1 = _softmax_update(s_ref[slot, 1], v_ref[1, pl.ds(r, tk), :], m1, a1, c)
            return m0, a0, m1, a1

        def body(t, carry):
            r0 = pl.multiple_of(2 * t * tk, tk)
            r1 = pl.multiple_of(r0 + tk, tk)
            r2 = pl.multiple_of(r1 + tk, tk)
            put(1, r1)
            carry = step(0, r0, carry)
            put(0, r2)
            return step(1, r1, carry)

        put(0, 0)
        paired = (n - 1) // 2
        carry = lax.fori_loop(0, paired, body, init + init)
        for t in range(2 * paired, n):
            if t + 1 < n:
                put((t + 1) % 2, (t + 1) * tk)
            carry = step(t % 2, t * tk, carry)
        finish(carry[1], carry[3])


def _kv_head(h):
    group = GQA_HEADS // GQA_KV_HEADS
    return jnp.where(h < MLA_HEADS, h, MLA_HEADS + (h - MLA_HEADS) // group)


def _attention(q, k, v, n_ctx, tile_off):
    B, _, S, _ = q.shape
    nq = S // TQ - tile_off
    tk = next(t for t in range(TK_MAX, 0, -LANES) if S % t == 0)
    kspec = lambda i: pl.BlockSpec((None, None, S, LANES), lambda b, hp, j: (b, _kv_head(2 * hp + i), 0, 0))
    return pl.pallas_call(
        functools.partial(_attn_kernel, tile_off=tile_off, n_ctx=n_ctx, n_all=S, tk=tk),
        grid=(B, N_PAIRS, nq),
        in_specs=[pl.BlockSpec((None, 2, TQ, LANES), lambda b, hp, j: (b, hp, j + tile_off, 0)),
                  kspec(0), kspec(1),
                  pl.BlockSpec((None, 2, S, LANES), lambda b, hp, j: (b, hp, 0, 0))],
        out_specs=pl.BlockSpec((None, TQ, LANES), lambda b, hp, j: (b, j, hp)),
        out_shape=jax.ShapeDtypeStruct((B, nq * TQ, N_PAIRS * LANES), F32),
        scratch_shapes=[pltpu.VMEM((2, 2, TQ, tk), F32)],
        compiler_params=_cparams(("parallel", "parallel", "arbitrary")),
        name="attention",
    )(q, k, k, v)


def _mix_kernel(o_ref, up_ref, uc_ref, un_ref, x_ref, mod_ref, cw_ref, cb_ref, lg_ref, lb_ref,
                mg_ref, wout_ref, n2_ref, rw_ref,
                xo_ref, h_ref, aff_ref, ext_ref, *, tile_off, last_tile):
    j = pl.program_id(1) + tile_off
    has_prev = j >= 2
    has_next = (j >= 1) & (j < last_tile)
    ext_ref[0:HALO, :] = jnp.where(has_prev, up_ref[TM - HALO:TM, :], 0.0)
    ext_ref[HALO:HALO + TM, :] = uc_ref[...]
    ext_ref[HALO + TM:HALO + TM + HALO, :] = jnp.where(has_next, un_ref[0:HALO, :], 0.0)
    y = jnp.zeros((TM, CONV_CH), F32) + cb_ref[...]
    for t in range(CONV_WIDTH):
        r = HALO - CONV_WIDTH // 2 + t
        y = y + ext_ref[r:r + TM, :] * cw_ref[t:t + 1, :]
    mu = jnp.mean(y, axis=-1, keepdims=True)
    yc = y - mu
    yn = yc * lax.rsqrt(jnp.mean(yc * yc, axis=-1, keepdims=True) + EPS) * lg_ref[...] + lb_ref[...]
    o_conv = yn * jax.nn.sigmoid(yn)

    o = o_ref[...]
    w_mla = MLA_HEADS * MLA_V
    w_att = w_mla + GQA_HEADS * GQA_HEAD_DIM
    cat = jnp.concatenate([
        _rms(o[:, :w_mla], mg_ref[:, :w_mla]).astype(BF16),
        _rms(o[:, w_mla:w_att], mg_ref[:, w_mla:w_att]).astype(BF16),
        _rms(o_conv, mg_ref[:, w_att:]).astype(BF16)], axis=-1)
    mix = jnp.dot(cat, wout_ref[...], preferred_element_type=F32)
    x = x_ref[...] + mod_ref[2:3, :] * mix
    xo_ref[...] = x
    hm = _rms(x, n2_ref[...]) * (1.0 + mod_ref[4:5, :]) + mod_ref[3:4, :]
    h_ref[...] = hm.astype(BF16)
    logits = lax.dot_general(rw_ref[...], hm, (((1,), (1,)), ((), ())),
                             preferred_element_type=F32, precision=HIGHEST)
    e = jnp.exp(logits - jnp.max(logits, axis=0, keepdims=True))
    aff_ref[...] = e / jnp.sum(e, axis=0, keepdims=True)


def _mix(o, u, xs, mod, lw, tile_off):
    B, S_o, _ = o.shape
    D = xs.shape[-1]
    nt_all = u.shape[1] // TM
    nt = S_o // TM
    const2 = lambda b, j: (0, 0)
    weights = [lw["conv_w"], lw["conv_dw_b"], lw["conv_ln_g"], lw["conv_ln_b"], lw["mix_out_g"], lw["w_out"],
               lw["norm2_g"], lw["router_wt"]]
    uspec = lambda d: pl.BlockSpec(
        (None, TM, CONV_CH), lambda b, j: (b, jnp.clip(j + tile_off + d, 0, nt_all - 1), 0))
    return pl.pallas_call(
        functools.partial(_mix_kernel, tile_off=tile_off, last_tile=nt_all - 1),
        grid=(B, nt),
        in_specs=[pl.BlockSpec((None, TM, o.shape[-1]), lambda b, j: (b, j, 0)),
                  uspec(-1), uspec(0), uspec(1),
                  pl.BlockSpec((None, TM, D), lambda b, j: (b, j + tile_off, 0)),
                  pl.BlockSpec((None, 6, D), lambda b, j: (_mod_row(b, j + tile_off), 0, 0))]
                 + [pl.BlockSpec(w.shape, const2) for w in weights],
        out_specs=[pl.BlockSpec((None, TM, D), lambda b, j: (b, j, 0)),
                   pl.BlockSpec((None, TM, D), lambda b, j: (b, j, 0)),
                   pl.BlockSpec((None, N_EXPERTS, TM), lambda b, j: (b, 0, j))],
        out_shape=[jax.ShapeDtypeStruct((B, S_o, D), F32),
                   jax.ShapeDtypeStruct((B, S_o, D), BF16),
                   jax.ShapeDtypeStruct((B, N_EXPERTS, S_o), F32)],
        scratch_shapes=[pltpu.VMEM((TM + 2 * HALO, CONV_CH), F32)],
        compiler_params=_cparams(("parallel", "parallel")),
        name="mix",
    )(o, u, u, u, xs, mod, *weights)


def _moe_kernel(x_ref, g_ref, w1_ref, w3_ref, w2_ref, y_ref, w1b, w3b, w2b, *, nchunk, rows):
    f = pl.program_id(1)
    w1b[...] = w1_ref[...].astype(BF16)
    w3b[...] = w3_ref[...].astype(BF16)
    w2b[...] = w2_ref[...].astype(BF16)

    def body(c, carry):
        r0 = pl.multiple_of(c * rows, rows)
        xb = x_ref[pl.ds(r0, rows), :]
        h1 = jnp.dot(xb, w1b[...], preferred_element_type=F32)
        h3 = jnp.dot(xb, w3b[...], preferred_element_type=F32)
        hid = (h1 * jax.nn.sigmoid(h1) * h3).astype(BF16)
        part = jnp.dot(hid, w2b[...], preferred_element_type=F32)

        @pl.when(f == 0)
        def _():
            y_ref[pl.ds(r0, rows), :] = part

        @pl.when(f > 0)
        def _():
            y_ref[pl.ds(r0, rows), :] += part
        return carry

    lax.fori_loop(0, nchunk, body, 0)

    @pl.when(f == pl.num_programs(1) - 1)
    def _():
        y_ref[...] = y_ref[...] * g_ref[...]


def _moe_tiles(R, FF):
    rows = next(r for r in range(512, 15, -16) if R % r == 0)
    tf = next(t for t in range(512, LANES - 1, -LANES) if FF % t == 0)
    return rows, tf


def _experts(xg, gate, w1, w3, w2, l):
    E, R, D = xg.shape
    FF = w1.shape[-1]
    rows, tf = _moe_tiles(R, FF)
    return pl.pallas_call(
        functools.partial(_moe_kernel, nchunk=R // rows, rows=rows),
        grid=(E, FF // tf),
        in_specs=[pl.BlockSpec((None, R, D), lambda e, f: (e, 0, 0)),
                  pl.BlockSpec((None, R, 1), lambda e, f: (e, 0, 0)),
                  pl.BlockSpec((None, None, D, tf), lambda e, f: (l, e, 0, f)),
                  pl.BlockSpec((None, None, D, tf), lambda e, f: (l, e, 0, f)),
                  pl.BlockSpec((None, None, tf, D), lambda e, f: (l, e, f, 0))],
        out_specs=pl.BlockSpec((None, R, D), lambda e, f: (e, 0, 0)),
        out_shape=jax.ShapeDtypeStruct((E, R, D), F32),
        scratch_shapes=[pltpu.VMEM((D, tf), BF16), pltpu.VMEM((D, tf), BF16), pltpu.VMEM((tf, D), BF16)],
        compiler_params=_cparams(("parallel", "arbitrary")),
        name="experts",
    )(xg, gate, w1, w3, w2)


def _final_kernel(x_ref, moe_ref, mod_ref, g_ref, o_ref):
    x = x_ref[...] + mod_ref[5:6, :] * moe_ref[...]
    o_ref[...] = _rms(x, g_ref[...])


def _final(x, moe, mod, g):
    B, N, D = x.shape
    tok = lambda b, j: (b, j, 0)
    return pl.pallas_call(
        _final_kernel,
        grid=(B, N // TM),
        in_specs=[pl.BlockSpec((None, TM, D), tok), pl.BlockSpec((None, TM, D), tok),
                  pl.BlockSpec((None, 6, D), lambda b, j: (b, 0, 0)),
                  pl.BlockSpec((1, D), lambda b, j: (0, 0))],
        out_specs=pl.BlockSpec((None, TM, D), tok),
        out_shape=jax.ShapeDtypeStruct((B, N, D), F32),
        compiler_params=_cparams(("parallel", "parallel")),
        name="final_norm",
    )(x, moe, mod, g)


def _pad_cols(w, lo, width=LANES):
    return jnp.pad(w, ((0, 0), (lo, width - lo - w.shape[1])))


def _prep_layer(p):
    w_in = p["w_in"]
    o = 0
    parts = {}
    for name, size in (("cq", MLA_Q_RANK), ("ckv", MLA_KV_RANK), ("kr", MLA_ROPE),
                       ("zq", GQA_HEADS * GQA_HEAD_DIM), ("zk", GQA_KV_HEADS * GQA_HEAD_DIM),
                       ("zv", GQA_KV_HEADS * GQA_HEAD_DIM), ("zc", 2 * CONV_CH)):
        parts[name] = w_in[:, o:o + size]
        o += size
    hd = GQA_HEAD_DIM
    group = GQA_HEADS // GQA_KV_HEADS
    cols = [parts["cq"], parts["ckv"]]
    cols += [_pad_cols(parts["zq"][:, h * hd:(h + 1) * hd], 0) for h in range(GQA_HEADS)]
    cols += [_pad_cols(parts["zk"][:, h * hd:(h + 1) * hd], 0) for h in range(GQA_KV_HEADS)]
    cols += [_pad_cols(parts["zv"][:, (h // group) * hd:(h // group + 1) * hd], (h % 2) * hd)
             for h in range(GQA_HEADS)]
    cols += [parts["zc"], _pad_cols(parts["kr"], MLA_NOPE)]
    w_in_p = jnp.concatenate(cols, axis=1).astype(BF16)

    dq = MLA_NOPE + MLA_ROPE
    w_uq = jnp.concatenate([_pad_cols(p["mla_w_uq"][:, h * dq:(h + 1) * dq], 0) for h in range(MLA_HEADS)],
                           axis=1).astype(BF16)
    dkv = MLA_NOPE + MLA_V
    wk = [_pad_cols(p["mla_w_ukv"][:, h * dkv:h * dkv + MLA_NOPE], 0) for h in range(MLA_HEADS)]
    wv = [_pad_cols(p["mla_w_ukv"][:, h * dkv + MLA_NOPE:(h + 1) * dkv], (h % 2) * MLA_V) for h in range(MLA_HEADS)]
    w_ukv = jnp.concatenate(wk + wv, axis=1).astype(BF16)

    row = lambda a: a.reshape(1, -1)
    return {
        "norm1_g": row(p["norm1_g"]), "w_in": w_in_p,
        "mla_q_norm_g": row(p["mla_q_norm_g"]), "w_uq": w_uq,
        "mla_kv_norm_g": row(p["mla_kv_norm_g"]), "w_ukv": w_ukv,
        "gqa_q_norm_g": _pad_cols(row(p["gqa_q_norm_g"]), 0), "gqa_k_norm_g": _pad_cols(row(p["gqa_k_norm_g"]), 0),
        "conv_w": jnp.pad(p["conv_dw_w"], ((0, 32 - CONV_WIDTH), (0, 0))),
        "conv_dw_b": row(p["conv_dw_b"]), "conv_ln_g": row(p["conv_ln_g"]), "conv_ln_b": row(p["conv_ln_b"]),
        "mix_out_g": row(p["mix_out_g"]), "w_out": p["w_out"].astype(BF16),
        "norm2_g": row(p["norm2_g"]), "router_wt": p["router_w"].T,
    }


def _rope_tables(n, n_ctx):
    rows = n // GRID_W
    row = jnp.repeat(jnp.arange(rows), GRID_W).astype(F32)
    col = jnp.tile(jnp.arange(GRID_W), rows).astype(F32)
    tabs = []
    for d_rot, lo in ((MLA_ROPE, MLA_NOPE), (GQA_HEAD_DIM, 0)):
        d_axis = d_rot // 2
        inv = ROPE_BASE ** (-jnp.arange(0, d_axis, 2, dtype=F32) / d_axis)
        ang = jnp.concatenate([row[:, None] * inv, col[:, None] * inv], axis=-1)
        cos, sin = jnp.cos(ang), jnp.sin(ang)
        cos_b = jnp.pad(jnp.concatenate([cos, cos], axis=1) - 1.0, ((n_ctx, 0), (lo, LANES - lo - d_rot))) + 1.0
        sin_b = jnp.pad(jnp.concatenate([-sin, sin], axis=1), ((n_ctx, 0), (lo, LANES - lo - d_rot)))
        tabs += [cos_b, sin_b]
    return tabs


def _route(aff, stride, cap, row0):
    gate, idx = lax.top_k(aff, cap)
    B = aff.shape[0]
    gidx = idx + (row0 + jnp.arange(B, dtype=idx.dtype) * stride)[:, None, None]
    E = aff.shape[1]
    return gidx.transpose(1, 0, 2).reshape(E, B * cap), gate.transpose(1, 0, 2).reshape(E, B * cap)


def _moe_layer(aff, h, n_ctx, with_ctx, w1, w3, w2, l):
    B, S_o, D = h.shape
    lat0 = n_ctx if with_ctx else 0
    n = S_o - lat0
    sets = [_route(aff[:, :, lat0:], S_o, CAPACITY_FACTOR * n // N_EXPERTS, lat0)]
    if with_ctx:
        sets.append(_route(aff[:, :, :n_ctx], S_o, CAPACITY_FACTOR * n_ctx // N_EXPERTS, 0))
    gidx = jnp.concatenate([s[0] for s in sets], axis=1)
    gate = jnp.concatenate([s[1] for s in sets], axis=1)
    h_flat = h.reshape(B * S_o, D)
    xg = h_flat[gidx]
    y = _experts(xg, gate[..., None], w1, w3, w2, l)
    out = jnp.zeros((B * S_o, D), F32).at[gidx.reshape(-1)].add(y.reshape(-1, D))
    return out.reshape(B, S_o, D)


def kernel(x, c, ctx, c_ctx, mod_w, mod_b, norm1_g, w_in, mla_q_norm_g, mla_w_uq, mla_kv_norm_g, mla_w_ukv,
           gqa_q_norm_g, gqa_k_norm_g, conv_dw_w, conv_dw_b, conv_ln_g, conv_ln_b, mix_out_g, w_out, norm2_g,
           router_w, exp_w1, exp_w3, exp_w2, final_g):
    B, N, D = x.shape
    n_ctx = ctx.shape[1]
    depth = mod_w.shape[0]
    assert n_ctx == TM == TQ and N % TM == 0
    stacked = {"norm1_g": norm1_g, "w_in": w_in, "mla_q_norm_g": mla_q_norm_g, "mla_w_uq": mla_w_uq,
               "mla_kv_norm_g": mla_kv_norm_g, "mla_w_ukv": mla_w_ukv, "gqa_q_norm_g": gqa_q_norm_g,
               "gqa_k_norm_g": gqa_k_norm_g, "conv_dw_w": conv_dw_w, "conv_dw_b": conv_dw_b,
               "conv_ln_g": conv_ln_g, "conv_ln_b": conv_ln_b, "mix_out_g": mix_out_g, "w_out": w_out,
               "norm2_g": norm2_g, "router_w": router_w}

    assert B <= CTX_ROW
    cvec = jnp.zeros((MOD_ROWS, D), F32).at[:B].set(c).at[CTX_ROW].set(c_ctx)
    mods = _modulation(cvec, mod_w, mod_b).reshape(depth, MOD_ROWS, 6, D)
    tabs = _rope_tables(N, n_ctx)

    xs = jnp.concatenate([ctx, x], axis=1)
    moe, modp = None, None
    for l in range(depth):
        lw = _prep_layer({k: v[l] for k, v in stacked.items()})
        last = l == depth - 1
        tile_off = 1 if last else 0
        xs, (q, k, v, u) = _project(xs, moe, modp, mods[l], lw, tabs)
        o = _attention(q, k, v, n_ctx, tile_off)
        xs, h, aff = _mix(o, u, xs, mods[l], lw, tile_off)
        moe = _moe_layer(aff, h, n_ctx, not last, exp_w1, exp_w3, exp_w2, l)
        modp = mods[l]
    return _final(xs, moe, modp, final_g.reshape(1, D))
```

```python
import functools
import math

import jax
import jax.numpy as jnp
from jax import lax
from jax.experimental import pallas as pl
from jax.experimental.pallas import tpu as pltpu

F32 = jnp.float32
BF16 = jnp.bfloat16
HIGHEST = lax.Precision.HIGHEST

GRID_W = 64
ROPE_BASE = 10000.0
EPS = 1e-6

MLA_HEADS = 8
MLA_Q_RANK = 256
MLA_KV_RANK = 128
MLA_NOPE = 64
MLA_ROPE = 32
MLA_V = 64
GQA_HEADS = 4
GQA_KV_HEADS = 2
GQA_HEAD_DIM = 64
CONV_CH = 256
CONV_WIDTH = 31
N_EXPERTS = 16
CAPACITY_FACTOR = 2

MLA_SCALE = 1.0 / math.sqrt(MLA_NOPE + MLA_ROPE)
GQA_SCALE = 1.0 / math.sqrt(GQA_HEAD_DIM)

LANES = 128
N_QHEADS = MLA_HEADS + GQA_HEADS
N_KHEADS = MLA_HEADS + GQA_KV_HEADS
N_PAIRS = N_QHEADS // 2
TM = 256
TQ = 256
TK_MAX = 2816
ONES_LANE = (MLA_V, 0)
LOG2E = math.log2(math.e)
HALO = 16
VMEM_LIMIT = 56 * 1024 * 1024
MOD_ROWS = 8
CTX_ROW = MOD_ROWS - 1

C_CQ = 0
C_CKV = C_CQ + MLA_Q_RANK
C_ZQ = C_CKV + MLA_KV_RANK
C_ZK = C_ZQ + GQA_HEADS * LANES
C_ZV = C_ZK + GQA_KV_HEADS * LANES
C_ZA = C_ZV + GQA_HEADS * LANES
C_ZB = C_ZA + CONV_CH
C_KR = C_ZB + CONV_CH
IN_PAD = C_KR + LANES


def _cparams(sem):
    return pltpu.CompilerParams(dimension_semantics=sem, vmem_limit_bytes=VMEM_LIMIT)


def _rms(x, g):
    return x * lax.rsqrt(jnp.mean(x * x, axis=-1, keepdims=True) + EPS) * g


def _head_rms(x, g, width):
    return x * lax.rsqrt(jnp.sum(x * x, axis=-1, keepdims=True) * (1.0 / width) + EPS) * g


def _rope(x, cos, sin, lo, half):
    lane = lax.broadcasted_iota(jnp.int32, x.shape, 1)
    first = (lane >= lo) & (lane < lo + half)
    rot = jnp.where(first, pltpu.roll(x, LANES - half, 1), pltpu.roll(x, half, 1))
    return x * cos + rot * sin


def _ones_col(h):
    lane = lax.broadcasted_iota(jnp.int32, (1, LANES), 1)
    return (lane == ONES_LANE[h % 2]).astype(F32)


def _mod_kernel(c_ref, w_ref, b_ref, o_ref):
    c = c_ref[...]
    a = c * jax.nn.sigmoid(c)
    o_ref[...] = jnp.dot(a, w_ref[...], preferred_element_type=F32, precision=HIGHEST) + b_ref[...]


def _modulation(cvec, mod_w, mod_b):
    L, D, W = mod_w.shape
    tn = 1536
    return pl.pallas_call(
        _mod_kernel,
        grid=(L, W // tn),
        in_specs=[pl.BlockSpec((MOD_ROWS, D), lambda l, n: (0, 0)),
                  pl.BlockSpec((None, D, tn), lambda l, n: (l, 0, n)),
                  pl.BlockSpec((None, 1, tn), lambda l, n: (l, 0, n))],
        out_specs=pl.BlockSpec((None, MOD_ROWS, tn), lambda l, n: (l, 0, n)),
        out_shape=jax.ShapeDtypeStruct((L, MOD_ROWS, W), F32),
        compiler_params=_cparams(("arbitrary", "arbitrary")),
        name="modulation",
    )(cvec, mod_w, mod_b.reshape(L, 1, W))


def _proj_kernel(*refs, has_moe):
    if has_moe:
        x_ref, moe_ref, modp_ref, refs = refs[0], refs[1], refs[2], refs[3:]
    else:
        x_ref, refs = refs[0], refs[1:]
    (mod_ref, n1_ref, win_ref, gq_ref, wuq_ref, gkv_ref, wukv_ref, ggq_ref, ggk_ref,
     cm_ref, sm_ref, cg_ref, sg_ref) = refs[:13]
    outs = refs[13:]
    if has_moe:
        xo_ref, outs = outs[0], outs[1:]
    q_ref, k_ref, v_ref, u_ref = outs

    x = x_ref[...]
    if has_moe:
        x = x + modp_ref[5:6, :] * moe_ref[...]
        xo_ref[...] = x
    hm = _rms(x, n1_ref[...]) * (1.0 + mod_ref[1:2, :]) + mod_ref[0:1, :]
    z = jnp.dot(hm.astype(BF16), win_ref[...], preferred_element_type=F32)

    cos_m, sin_m, cos_g, sin_g = cm_ref[...], sm_ref[...], cg_ref[...], sg_ref[...]

    cqn = _rms(z[:, C_CQ:C_CQ + MLA_Q_RANK], gq_ref[...])
    q = jnp.dot(cqn.astype(BF16), wuq_ref[...], preferred_element_type=F32)
    for h in range(MLA_HEADS):
        q_ref[h] = _rope(q[:, h * LANES:(h + 1) * LANES], cos_m, sin_m, MLA_NOPE, MLA_ROPE // 2).astype(BF16)

    ckvn = _rms(z[:, C_CKV:C_CKV + MLA_KV_RANK], gkv_ref[...])
    kv = jnp.dot(ckvn.astype(BF16), wukv_ref[...], preferred_element_type=F32)
    kr = _rope(z[:, C_KR:C_KR + LANES], cos_m, sin_m, MLA_NOPE, MLA_ROPE // 2)
    for h in range(MLA_HEADS):
        k_ref[h] = (kv[:, h * LANES:(h + 1) * LANES] + kr).astype(BF16)
        v_ref[h] = (kv[:, (MLA_HEADS + h) * LANES:(MLA_HEADS + h + 1) * LANES] + _ones_col(h)).astype(BF16)

    for h in range(GQA_HEADS):
        zq = z[:, C_ZQ + h * LANES:C_ZQ + (h + 1) * LANES]
        q_ref[MLA_HEADS + h] = _rope(_head_rms(zq, ggq_ref[...], GQA_HEAD_DIM), cos_g, sin_g,
                                     0, GQA_HEAD_DIM // 2).astype(BF16)
        v_ref[MLA_HEADS + h] = (z[:, C_ZV + h * LANES:C_ZV + (h + 1) * LANES] + _ones_col(h)).astype(BF16)
    for h in range(GQA_KV_HEADS):
        zk = z[:, C_ZK + h * LANES:C_ZK + (h + 1) * LANES]
        k_ref[MLA_HEADS + h] = _rope(_head_rms(zk, ggk_ref[...], GQA_HEAD_DIM), cos_g, sin_g,
                                     0, GQA_HEAD_DIM // 2).astype(BF16)

    u_ref[...] = z[:, C_ZA:C_ZA + CONV_CH] * jax.nn.sigmoid(z[:, C_ZB:C_ZB + CONV_CH])


def _mod_row(b, j):
    return jnp.where(j == 0, CTX_ROW, b)


def _project(xs, moe, modp, mod, lw, tabs):
    B, S, D = xs.shape
    nt = S // TM
    has_moe = moe is not None
    tok = lambda b, j: (b, j, 0)
    const2 = lambda b, j: (0, 0)
    modspec = pl.BlockSpec((None, 6, D), lambda b, j: (_mod_row(b, j), 0, 0))
    in_specs = [pl.BlockSpec((None, TM, D), tok)]
    args = [xs]
    if has_moe:
        in_specs += [pl.BlockSpec((None, TM, D), tok), modspec]
        args += [moe, modp]
    weights = [lw["norm1_g"], lw["w_in"], lw["mla_q_norm_g"], lw["w_uq"], lw["mla_kv_norm_g"], lw["w_ukv"],
               lw["gqa_q_norm_g"], lw["gqa_k_norm_g"]]
    in_specs += [modspec] + [pl.BlockSpec(w.shape, const2) for w in weights]
    in_specs += [pl.BlockSpec((TM, LANES), lambda b, j: (j, 0))] * 4
    args += [mod] + weights + list(tabs)
    head_spec = lambda n: pl.BlockSpec((None, n, TM, LANES), lambda b, j: (b, 0, j, 0))
    out_specs = [head_spec(N_QHEADS), head_spec(N_KHEADS), head_spec(N_QHEADS),
                 pl.BlockSpec((None, TM, CONV_CH), tok)]
    out_shape = [jax.ShapeDtypeStruct((B, N_QHEADS, S, LANES), BF16),
                 jax.ShapeDtypeStruct((B, N_KHEADS, S, LANES), BF16),
                 jax.ShapeDtypeStruct((B, N_QHEADS, S, LANES), BF16),
                 jax.ShapeDtypeStruct((B, S, CONV_CH), F32)]
    if has_moe:
        out_specs = [pl.BlockSpec((None, TM, D), tok)] + out_specs
        out_shape = [jax.ShapeDtypeStruct((B, S, D), F32)] + out_shape
    outs = pl.pallas_call(
        functools.partial(_proj_kernel, has_moe=has_moe),
        grid=(B, nt), in_specs=in_specs, out_specs=out_specs, out_shape=out_shape,
        compiler_params=_cparams(("parallel", "parallel")),
        name="project",
    )(*args)
    if has_moe:
        return outs[0], outs[1:]
    return xs, outs


def _scores(q, k):
    return lax.dot_general(q, k, (((1,), (1,)), ((), ())), preferred_element_type=F32)


def _softmax_update(s, v, m, acc, c):
    m_new = jnp.maximum(m, jnp.max(s, axis=-1, keepdims=True))
    alpha = jnp.exp2((m - m_new) * c)
    p = jnp.exp2((s - m_new) * c).astype(BF16)
    acc = alpha * acc + jnp.dot(p, v, preferred_element_type=F32)
    return m_new, acc


def _attn_kernel(q_ref, k0_ref, k1_ref, v_ref, o_ref, s_ref, *, tile_off, n_ctx, n_all, tk):
    hp = pl.program_id(1)
    j = pl.program_id(2) + tile_off
    c = jnp.where(hp < MLA_HEADS // 2, MLA_SCALE * LOG2E, GQA_SCALE * LOG2E).astype(F32)
    q0, q1 = q_ref[0], q_ref[1]
    init = (jnp.full((TQ, 1), -jnp.inf, F32), jnp.zeros((TQ, LANES), F32))

    def finish(acc0, acc1):
        lane = lax.broadcasted_iota(jnp.int32, (TQ, LANES), 1)
        l0 = acc0[:, ONES_LANE[0]:ONES_LANE[0] + 1]
        l1 = acc1[:, ONES_LANE[1]:ONES_LANE[1] + 1]
        o_ref[...] = jnp.where(lane < LANES // 2, acc0 / l0, acc1 / l1)

    @pl.when(j == 0)
    def _():
        _, acc0 = _softmax_update(_scores(q0, k0_ref[0:n_ctx, :]), v_ref[0, 0:n_ctx, :], *init, c)
        _, acc1 = _softmax_update(_scores(q1, k1_ref[0:n_ctx, :]), v_ref[1, 0:n_ctx, :], *init, c)
        finish(acc0, acc1)

    @pl.when(j > 0)
    def _():
        n = n_all // tk

        def put(slot, r):
            s_ref[slot, 0] = _scores(q0, k0_ref[pl.ds(r, tk), :])
            s_ref[slot, 1] = _scores(q1, k1_ref[pl.ds(r, tk), :])

        def step(slot, r, carry):
            m0, a0, m1, a1 = carry
            m0, a0 = _softmax_update(s_ref[slot, 0], v_ref[0, pl.ds(r, tk), :], m0, a0, c)
            m1, a1 = _softmax_update(s_ref[slot, 1], v_ref[1, pl.ds(r, tk), :], m1, a1, c)
            return m0, a0, m1, a1

        def body(t, carry):
            r0 = pl.multiple_of(2 * t * tk, tk)
            r1 = pl.multiple_of(r0 + tk, tk)
            r2 = pl.multiple_of(r1 + tk, tk)
            put(1, r1)
            carry = step(0, r0, carry)
            put(0, r2)
            return step(1, r1, carry)

        put(0, 0)
        paired = (n - 1) // 2
        carry = lax.fori_loop(0, paired, body, init + init)
        for t in range(2 * paired, n):
            if t + 1 < n:
                put((t + 1) % 2, (t + 1) * tk)
            carry = step(t % 2, t * tk, carry)
        finish(carry[1], carry[3])


def _kv_head(h):
    group = GQA_HEADS // GQA_KV_HEADS
    return jnp.where(h < MLA_HEADS, h, MLA_HEADS + (h - MLA_HEADS) // group)


def _attention(q, k, v, n_ctx, tile_off):
    B, _, S, _ = q.shape
    nq = S // TQ - tile_off
    tk = next(t for t in range(TK_MAX, 0, -LANES) if S % t == 0)
    kspec = lambda i: pl.BlockSpec((None, None, S, LANES), lambda b, hp, j: (b, _kv_head(2 * hp + i), 0, 0))
    return pl.pallas_call(
        functools.partial(_attn_kernel, tile_off=tile_off, n_ctx=n_ctx, n_all=S, tk=tk),
        grid=(B, N_PAIRS, nq),
        in_specs=[pl.BlockSpec((None, 2, TQ, LANES), lambda b, hp, j: (b, hp, j + tile_off, 0)),
                  kspec(0), kspec(1),
                  pl.BlockSpec((None, 2, S, LANES), lambda b, hp, j: (b, hp, 0, 0))],
        out_specs=pl.BlockSpec((None, TQ, LANES), lambda b, hp, j: (b, j, hp)),
        out_shape=jax.ShapeDtypeStruct((B, nq * TQ, N_PAIRS * LANES), F32),
        scratch_shapes=[pltpu.VMEM((2, 2, TQ, tk), F32)],
        compiler_params=_cparams(("parallel", "parallel", "arbitrary")),
        name="attention",
    )(q, k, k, v)


def _mix_kernel(o_ref, up_ref, uc_ref, un_ref, x_ref, mod_ref, cw_ref, cb_ref, lg_ref, lb_ref,
                mg_ref, wout_ref, n2_ref, rw_ref,
                xo_ref, h_ref, aff_ref, ext_ref, *, tile_off, last_tile):
    j = pl.program_id(1) + tile_off
    has_prev = j >= 2
    has_next = (j >= 1) & (j < last_tile)
    ext_ref[0:HALO, :] = jnp.where(has_prev, up_ref[TM - HALO:TM, :], 0.0)
    ext_ref[HALO:HALO + TM, :] = uc_ref[...]
    ext_ref[HALO + TM:HALO + TM + HALO, :] = jnp.where(has_next, un_ref[0:HALO, :], 0.0)
    y = jnp.zeros((TM, CONV_CH), F32) + cb_ref[...]
    for t in range(CONV_WIDTH):
        r = HALO - CONV_WIDTH // 2 + t
        y = y + ext_ref[r:r + TM, :] * cw_ref[t:t + 1, :]
    mu = jnp.mean(y, axis=-1, keepdims=True)
    yc = y - mu
    yn = yc * lax.rsqrt(jnp.mean(yc * yc, axis=-1, keepdims=True) + EPS) * lg_ref[...] + lb_ref[...]
    o_conv = yn * jax.nn.sigmoid(yn)

    o = o_ref[...]
    w_mla = MLA_HEADS * MLA_V
    w_att = w_mla + GQA_HEADS * GQA_HEAD_DIM
    cat = jnp.concatenate([
        _rms(o[:, :w_mla], mg_ref[:, :w_mla]).astype(BF16),
        _rms(o[:, w_mla:w_att], mg_ref[:, w_mla:w_att]).astype(BF16),
        _rms(o_conv, mg_ref[:, w_att:]).astype(BF16)], axis=-1)
    mix = jnp.dot(cat, wout_ref[...], preferred_element_type=F32)
    x = x_ref[...] + mod_ref[2:3, :] * mix
    xo_ref[...] = x
    hm = _rms(x, n2_ref[...]) * (1.0 + mod_ref[4:5, :]) + mod_ref[3:4, :]
    h_ref[...] = hm.astype(BF16)
    logits = lax.dot_general(rw_ref[...], hm, (((1,), (1,)), ((), ())),
                             preferred_element_type=F32, precision=HIGHEST)
    e = jnp.exp(logits - jnp.max(logits, axis=0, keepdims=True))
    aff_ref[...] = e / jnp.sum(e, axis=0, keepdims=True)


def _mix(o, u, xs, mod, lw, tile_off):
    B, S_o, _ = o.shape
    D = xs.shape[-1]
    nt_all = u.shape[1] // TM
    nt = S_o // TM
    const2 = lambda b, j: (0, 0)
    weights = [lw["conv_w"], lw["conv_dw_b"], lw["conv_ln_g"], lw["conv_ln_b"], lw["mix_out_g"], lw["w_out"],
               lw["norm2_g"], lw["router_wt"]]
    uspec = lambda d: pl.BlockSpec(
        (None, TM, CONV_CH), lambda b, j: (b, jnp.clip(j + tile_off + d, 0, nt_all - 1), 0))
    return pl.pallas_call(
        functools.partial(_mix_kernel, tile_off=tile_off, last_tile=nt_all - 1),
        grid=(B, nt),
        in_specs=[pl.BlockSpec((None, TM, o.shape[-1]), lambda b, j: (b, j, 0)),
                  uspec(-1), uspec(0), uspec(1),
                  pl.BlockSpec((None, TM, D), lambda b, j: (b, j + tile_off, 0)),
                  pl.BlockSpec((None, 6, D), lambda b, j: (_mod_row(b, j + tile_off), 0, 0))]
                 + [pl.BlockSpec(w.shape, const2) for w in weights],
        out_specs=[pl.BlockSpec((None, TM, D), lambda b, j: (b, j, 0)),
                   pl.BlockSpec((None, TM, D), lambda b, j: (b, j, 0)),
                   pl.BlockSpec((None, N_EXPERTS, TM), lambda b, j: (b, 0, j))],
        out_shape=[jax.ShapeDtypeStruct((B, S_o, D), F32),
                   jax.ShapeDtypeStruct((B, S_o, D), BF16),
                   jax.ShapeDtypeStruct((B, N_EXPERTS, S_o), F32)],
        scratch_shapes=[pltpu.VMEM((TM + 2 * HALO, CONV_CH), F32)],
        compiler_params=_cparams(("parallel", "parallel")),
        name="mix",
    )(o, u, u, u, xs, mod, *weights)


def _moe_kernel(x_ref, g_ref, w1_ref, w3_ref, w2_ref, y_ref, w1b, w3b, w2b, *, nchunk, rows):
    f = pl.program_id(1)
    w1b[...] = w1_ref[...].astype(BF16)
    w3b[...] = w3_ref[...].astype(BF16)
    w2b[...] = w2_ref[...].astype(BF16)

    def body(c, carry):
        r0 = pl.multiple_of(c * rows, rows)
        xb = x_ref[pl.ds(r0, rows), :]
        h1 = jnp.dot(xb, w1b[...], preferred_element_type=F32)
        h3 = jnp.dot(xb, w3b[...], preferred_element_type=F32)
        hid = (h1 * jax.nn.sigmoid(h1) * h3).astype(BF16)
        part = jnp.dot(hid, w2b[...], preferred_element_type=F32)

        @pl.when(f == 0)
        def _():
            y_ref[pl.ds(r0, rows), :] = part

        @pl.when(f > 0)
        def _():
            y_ref[pl.ds(r0, rows), :] += part
        return carry

    lax.fori_loop(0, nchunk, body, 0)

    @pl.when(f == pl.num_programs(1) - 1)
    def _():
        y_ref[...] = y_ref[...] * g_ref[...]


def _moe_tiles(R, FF):
    rows = next(r for r in range(512, 15, -16) if R % r == 0)
    tf = next(t for t in range(512, LANES - 1, -LANES) if FF % t == 0)
    return rows, tf


def _experts(xg, gate, w1, w3, w2, l):
    E, R, D = xg.shape
    FF = w1.shape[-1]
    rows, tf = _moe_tiles(R, FF)
    return pl.pallas_call(
        functools.partial(_moe_kernel, nchunk=R // rows, rows=rows),
        grid=(E, FF // tf),
        in_specs=[pl.BlockSpec((None, R, D), lambda e, f: (e, 0, 0)),
                  pl.BlockSpec((None, R, 1), lambda e, f: (e, 0, 0)),
                  pl.BlockSpec((None, None, D, tf), lambda e, f: (l, e, 0, f)),
                  pl.BlockSpec((None, None, D, tf), lambda e, f: (l, e, 0, f)),
                  pl.BlockSpec((None, None, tf, D), lambda e, f: (l, e, f, 0))],
        out_specs=pl.BlockSpec((None, R, D), lambda e, f: (e, 0, 0)),
        out_shape=jax.ShapeDtypeStruct((E, R, D), F32),
        scratch_shapes=[pltpu.VMEM((D, tf), BF16), pltpu.VMEM((D, tf), BF16), pltpu.VMEM((tf, D), BF16)],
        compiler_params=_cparams(("parallel", "arbitrary")),
        name="experts",
    )(xg, gate, w1, w3, w2)


def _final_kernel(x_ref, moe_ref, mod_ref, g_ref, o_ref):
    x = x_ref[...] + mod_ref[5:6, :] * moe_ref[...]
    o_ref[...] = _rms(x, g_ref[...])


def _final(x, moe, mod, g):
    B, N, D = x.shape
    tok = lambda b, j: (b, j, 0)
    return pl.pallas_call(
        _final_kernel,
        grid=(B, N // TM),
        in_specs=[pl.BlockSpec((None, TM, D), tok), pl.BlockSpec((None, TM, D), tok),
                  pl.BlockSpec((None, 6, D), lambda b, j: (b, 0, 0)),
                  pl.BlockSpec((1, D), lambda b, j: (0, 0))],
        out_specs=pl.BlockSpec((None, TM, D), tok),
        out_shape=jax.ShapeDtypeStruct((B, N, D), F32),
        compiler_params=_cparams(("parallel", "parallel")),
        name="final_norm",
    )(x, moe, mod, g)


def _pad_cols(w, lo, width=LANES):
    return jnp.pad(w, ((0, 0), (lo, width - lo - w.shape[1])))


def _prep_layer(p):
    w_in = p["w_in"]
    o = 0
    parts = {}
    for name, size in (("cq", MLA_Q_RANK), ("ckv", MLA_KV_RANK), ("kr", MLA_ROPE),
                       ("zq", GQA_HEADS * GQA_HEAD_DIM), ("zk", GQA_KV_HEADS * GQA_HEAD_DIM),
                       ("zv", GQA_KV_HEADS * GQA_HEAD_DIM), ("zc", 2 * CONV_CH)):
        parts[name] = w_in[:, o:o + size]
        o += size
    hd = GQA_HEAD_DIM
    group = GQA_HEADS // GQA_KV_HEADS
    cols = [parts["cq"], parts["ckv"]]
    cols += [_pad_cols(parts["zq"][:, h * hd:(h + 1) * hd], 0) for h in range(GQA_HEADS)]
    cols += [_pad_cols(parts["zk"][:, h * hd:(h + 1) * hd], 0) for h in range(GQA_KV_HEADS)]
    cols += [_pad_cols(parts["zv"][:, (h // group) * hd:(h // group + 1) * hd], (h % 2) * hd)
             for h in range(GQA_HEADS)]
    cols += [parts["zc"], _pad_cols(parts["kr"], MLA_NOPE)]
    w_in_p = jnp.concatenate(cols, axis=1).astype(BF16)

    dq = MLA_NOPE + MLA_ROPE
    w_uq = jnp.concatenate([_pad_cols(p["mla_w_uq"][:, h * dq:(h + 1) * dq], 0) for h in range(MLA_HEADS)],
                           axis=1).astype(BF16)
    dkv = MLA_NOPE + MLA_V
    wk = [_pad_cols(p["mla_w_ukv"][:, h * dkv:h * dkv + MLA_NOPE], 0) for h in range(MLA_HEADS)]
    wv = [_pad_cols(p["mla_w_ukv"][:, h * dkv + MLA_NOPE:(h + 1) * dkv], (h % 2) * MLA_V) for h in range(MLA_HEADS)]
    w_ukv = jnp.concatenate(wk + wv, axis=1).astype(BF16)

    row = lambda a: a.reshape(1, -1)
    return {
        "norm1_g": row(p["norm1_g"]), "w_in": w_in_p,
        "mla_q_norm_g": row(p["mla_q_norm_g"]), "w_uq": w_uq,
        "mla_kv_norm_g": row(p["mla_kv_norm_g"]), "w_ukv": w_ukv,
        "gqa_q_norm_g": _pad_cols(row(p["gqa_q_norm_g"]), 0), "gqa_k_norm_g": _pad_cols(row(p["gqa_k_norm_g"]), 0),
        "conv_w": jnp.pad(p["conv_dw_w"], ((0, 32 - CONV_WIDTH), (0, 0))),
        "conv_dw_b": row(p["conv_dw_b"]), "conv_ln_g": row(p["conv_ln_g"]), "conv_ln_b": row(p["conv_ln_b"]),
        "mix_out_g": row(p["mix_out_g"]), "w_out": p["w_out"].astype(BF16),
        "norm2_g": row(p["norm2_g"]), "router_wt": p["router_w"].T,
    }


def _rope_tables(n, n_ctx):
    rows = n // GRID_W
    row = jnp.repeat(jnp.arange(rows), GRID_W).astype(F32)
    col = jnp.tile(jnp.arange(GRID_W), rows).astype(F32)
    tabs = []
    for d_rot, lo in ((MLA_ROPE, MLA_NOPE), (GQA_HEAD_DIM, 0)):
        d_axis = d_rot // 2
        inv = ROPE_BASE ** (-jnp.arange(0, d_axis, 2, dtype=F32) / d_axis)
        ang = jnp.concatenate([row[:, None] * inv, col[:, None] * inv], axis=-1)
        cos, sin = jnp.cos(ang), jnp.sin(ang)
        cos_b = jnp.pad(jnp.concatenate([cos, cos], axis=1) - 1.0, ((n_ctx, 0), (lo, LANES - lo - d_rot))) + 1.0
        sin_b = jnp.pad(jnp.concatenate([-sin, sin], axis=1), ((n_ctx, 0), (lo, LANES - lo - d_rot)))
        tabs += [cos_b, sin_b]
    return tabs


def _route(aff, stride, cap, row0):
    gate, idx = lax.top_k(aff, cap)
    B = aff.shape[0]
    gidx = idx + (row0 + jnp.arange(B, dtype=idx.dtype) * stride)[:, None, None]
    E = aff.shape[1]
    return gidx.transpose(1, 0, 2).reshape(E, B * cap), gate.transpose(1, 0, 2).reshape(E, B * cap)


def _moe_layer(aff, h, n_ctx, with_ctx, w1, w3, w2, l):
    B, S_o, D = h.shape
    lat0 = n_ctx if with_ctx else 0
    n = S_o - lat0
    sets = [_route(aff[:, :, lat0:], S_o, CAPACITY_FACTOR * n // N_EXPERTS, lat0)]
    if with_ctx:
        sets.append(_route(aff[:, :, :n_ctx], S_o, CAPACITY_FACTOR * n_ctx // N_EXPERTS, 0))
    gidx = jnp.concatenate([s[0] for s in sets], axis=1)
    gate = jnp.concatenate([s[1] for s in sets], axis=1)
    h_flat = h.reshape(B * S_o, D)
    xg = h_flat[gidx]
    y = _experts(xg, gate[..., None], w1, w3, w2, l)
    out = jnp.zeros((B * S_o, D), F32).at[gidx.reshape(-1)].add(y.reshape(-1, D))
    return out.reshape(B, S_o, D)


def kernel(x, c, ctx, c_ctx, mod_w, mod_b, norm1_g, w_in, mla_q_norm_g, mla_w_uq, mla_kv_norm_g, mla_w_ukv,
           gqa_q_norm_g, gqa_k_norm_g, conv_dw_w, conv_dw_b, conv_ln_g, conv_ln_b, mix_out_g, w_out, norm2_g,
           router_w, exp_w1, exp_w3, exp_w2, final_g):
    B, N, D = x.shape
    n_ctx = ctx.shape[1]
    depth = mod_w.shape[0]
    assert n_ctx == TM == TQ and N % TM == 0
    stacked = {"norm1_g": norm1_g, "w_in": w_in, "mla_q_norm_g": mla_q_norm_g, "mla_w_uq": mla_w_uq,
               "mla_kv_norm_g": mla_kv_norm_g, "mla_w_ukv": mla_w_ukv, "gqa_q_norm_g": gqa_q_norm_g,
               "gqa_k_norm_g": gqa_k_norm_g, "conv_dw_w": conv_dw_w, "conv_dw_b": conv_dw_b,
               "conv_ln_g": conv_ln_g, "conv_ln_b": conv_ln_b, "mix_out_g": mix_out_g, "w_out": w_out,
               "norm2_g": norm2_g, "router_w": router_w}

    assert B <= CTX_ROW
    cvec = jnp.zeros((MOD_ROWS, D), F32).at[:B].set(c).at[CTX_ROW].set(c_ctx)
    mods = _modulation(cvec, mod_w, mod_b).reshape(depth, MOD_ROWS, 6, D)
    tabs = _rope_tables(N, n_ctx)

    xs = jnp.concatenate([ctx, x], axis=1)
    moe, modp = None, None
    for l in range(depth):
        lw = _prep_layer({k: v[l] for k, v in stacked.items()})
        last = l == depth - 1
        tile_off = 1 if last else 0
        xs, (q, k, v, u) = _project(xs, moe, modp, mods[l], lw, tabs)
        o = _attention(q, k, v, n_ctx, tile_off)
        xs, h, aff = _mix(o, u, xs, mods[l], lw, tile_off)
        moe = _moe_layer(aff, h, n_ctx, not last, exp_w1, exp_w3, exp_w2, l)
        modp = mods[l]
    return _final(xs, moe, modp, final_g.reshape(1, D))
```

```python
import functools
import math

import jax
import jax.numpy as jnp
from jax import lax
from jax.experimental import pallas as pl
from jax.experimental.pallas import tpu as pltpu

F32 = jnp.float32
BF16 = jnp.bfloat16
HIGHEST = lax.Precision.HIGHEST

GRID_W = 64
ROPE_BASE = 10000.0
EPS = 1e-6

MLA_HEADS = 8
MLA_Q_RANK = 256
MLA_KV_RANK = 128
MLA_NOPE = 64
MLA_ROPE = 32
MLA_V = 64
GQA_HEADS = 4
GQA_KV_HEADS = 2
GQA_HEAD_DIM = 64
CONV_CH = 256
CONV_WIDTH = 31
N_EXPERTS = 16
CAPACITY_FACTOR = 2

MLA_SCALE = 1.0 / math.sqrt(MLA_NOPE + MLA_ROPE)
GQA_SCALE = 1.0 / math.sqrt(GQA_HEAD_DIM)

LANES = 128
N_QHEADS = MLA_HEADS + GQA_HEADS
N_KHEADS = MLA_HEADS + GQA_KV_HEADS
N_PAIRS = N_QHEADS // 2
TM = 256
TQ = 256
TK_MAX = 2816
ONES_LANE = (MLA_V, 0)
LOG2E = math.log2(math.e)
HALO = 16
VMEM_LIMIT = 56 * 1024 * 1024
MOD_ROWS = 8
CTX_ROW = MOD_ROWS - 1

C_CQ = 0
C_CKV = C_CQ + MLA_Q_RANK
C_ZQ = C_CKV + MLA_KV_RANK
C_ZK = C_ZQ + GQA_HEADS * LANES
C_ZV = C_ZK + GQA_KV_HEADS * LANES
C_ZA = C_ZV + GQA_HEADS * LANES
C_ZB = C_ZA + CONV_CH
C_KR = C_ZB + CONV_CH
IN_PAD = C_KR + LANES


def _cparams(sem):
    return pltpu.CompilerParams(dimension_semantics=sem, vmem_limit_bytes=VMEM_LIMIT)


def _rms(x, g):
    return x * lax.rsqrt(jnp.mean(x * x, axis=-1, keepdims=True) + EPS) * g


def _head_rms(x, g, width):
    return x * lax.rsqrt(jnp.sum(x * x, axis=-1, keepdims=True) * (1.0 / width) + EPS) * g


def _rope(x, cos, sin, lo, half):
    lane = lax.broadcasted_iota(jnp.int32, x.shape, 1)
    first = (lane >= lo) & (lane < lo + half)
    rot = jnp.where(first, pltpu.roll(x, LANES - half, 1), pltpu.roll(x, half, 1))
    return x * cos + rot * sin


def _ones_col(h):
    lane = lax.broadcasted_iota(jnp.int32, (1, LANES), 1)
    return (lane == ONES_LANE[h % 2]).astype(F32)


def _mod_kernel(c_ref, w_ref, b_ref, o_ref):
    c = c_ref[...]
    a = c * jax.nn.sigmoid(c)
    o_ref[...] = jnp.dot(a, w_ref[...], preferred_element_type=F32, precision=HIGHEST) + b_ref[...]


def _modulation(cvec, mod_w, mod_b):
    L, D, W = mod_w.shape
    tn = 1536
    return pl.pallas_call(
        _mod_kernel,
        grid=(L, W // tn),
        in_specs=[pl.BlockSpec((MOD_ROWS, D), lambda l, n: (0, 0)),
                  pl.BlockSpec((None, D, tn), lambda l, n: (l, 0, n)),
                  pl.BlockSpec((None, 1, tn), lambda l, n: (l, 0, n))],
        out_specs=pl.BlockSpec((None, MOD_ROWS, tn), lambda l, n: (l, 0, n)),
        out_shape=jax.ShapeDtypeStruct((L, MOD_ROWS, W), F32),
        compiler_params=_cparams(("arbitrary", "arbitrary")),
        name="modulation",
    )(cvec, mod_w, mod_b.reshape(L, 1, W))


def _proj_kernel(*refs, has_moe):
    if has_moe:
        x_ref, moe_ref, modp_ref, refs = refs[0], refs[1], refs[2], refs[3:]
    else:
        x_ref, refs = refs[0], refs[1:]
    (mod_ref, n1_ref, win_ref, gq_ref, wuq_ref, gkv_ref, wukv_ref, ggq_ref, ggk_ref,
     cm_ref, sm_ref, cg_ref, sg_ref) = refs[:13]
    outs = refs[13:]
    if has_moe:
        xo_ref, outs = outs[0], outs[1:]
    q_ref, k_ref, v_ref, u_ref = outs

    x = x_ref[...]
    if has_moe:
        x = x + modp_ref[5:6, :] * moe_ref[...]
        xo_ref[...] = x
    hm = _rms(x, n1_ref[...]) * (1.0 + mod_ref[1:2, :]) + mod_ref[0:1, :]
    z = jnp.dot(hm.astype(BF16), win_ref[...], preferred_element_type=F32)

    cos_m, sin_m, cos_g, sin_g = cm_ref[...], sm_ref[...], cg_ref[...], sg_ref[...]

    cqn = _rms(z[:, C_CQ:C_CQ + MLA_Q_RANK], gq_ref[...])
    q = jnp.dot(cqn.astype(BF16), wuq_ref[...], preferred_element_type=F32)
    for h in range(MLA_HEADS):
        q_ref[h] = _rope(q[:, h * LANES:(h + 1) * LANES], cos_m, sin_m, MLA_NOPE, MLA_ROPE // 2).astype(BF16)

    ckvn = _rms(z[:, C_CKV:C_CKV + MLA_KV_RANK], gkv_ref[...])
    kv = jnp.dot(ckvn.astype(BF16), wukv_ref[...], preferred_element_type=F32)
    kr = _rope(z[:, C_KR:C_KR + LANES], cos_m, sin_m, MLA_NOPE, MLA_ROPE // 2)
    for h in range(MLA_HEADS):
        k_ref[h] = (kv[:, h * LANES:(h + 1) * LANES] + kr).astype(BF16)
        v_ref[h] = (kv[:, (MLA_HEADS + h) * LANES:(MLA_HEADS + h + 1) * LANES] + _ones_col(h)).astype(BF16)

    for h in range(GQA_HEADS):
        zq = z[:, C_ZQ + h * LANES:C_ZQ + (h + 1) * LANES]
        q_ref[MLA_HEADS + h] = _rope(_head_rms(zq, ggq_ref[...], GQA_HEAD_DIM), cos_g, sin_g,
                                     0, GQA_HEAD_DIM // 2).astype(BF16)
        v_ref[MLA_HEADS + h] = (z[:, C_ZV + h * LANES:C_ZV + (h + 1) * LANES] + _ones_col(h)).astype(BF16)
    for h in range(GQA_KV_HEADS):
        zk = z[:, C_ZK + h * LANES:C_ZK + (h + 1) * LANES]
        k_ref[MLA_HEADS + h] = _rope(_head_rms(zk, ggk_ref[...], GQA_HEAD_DIM), cos_g, sin_g,
                                     0, GQA_HEAD_DIM // 2).astype(BF16)

    u_ref[...] = z[:, C_ZA:C_ZA + CONV_CH] * jax.nn.sigmoid(z[:, C_ZB:C_ZB + CONV_CH])


def _mod_row(b, j):
    return jnp.where(j == 0, CTX_ROW, b)


def _project(xs, moe, modp, mod, lw, tabs):
    B, S, D = xs.shape
    nt = S // TM
    has_moe = moe is not None
    tok = lambda b, j: (b, j, 0)
    const2 = lambda b, j: (0, 0)
    modspec = pl.BlockSpec((None, 6, D), lambda b, j: (_mod_row(b, j), 0, 0))
    in_specs = [pl.BlockSpec((None, TM, D), tok)]
    args = [xs]
    if has_moe:
        in_specs += [pl.BlockSpec((None, TM, D), tok), modspec]
        args += [moe, modp]
    weights = [lw["norm1_g"], lw["w_in"], lw["mla_q_norm_g"], lw["w_uq"], lw["mla_kv_norm_g"], lw["w_ukv"],
               lw["gqa_q_norm_g"], lw["gqa_k_norm_g"]]
    in_specs += [modspec] + [pl.BlockSpec(w.shape, const2) for w in weights]
    in_specs += [pl.BlockSpec((TM, LANES), lambda b, j: (j, 0))] * 4
    args += [mod] + weights + list(tabs)
    head_spec = lambda n: pl.BlockSpec((None, n, TM, LANES), lambda b, j: (b, 0, j, 0))
    out_specs = [head_spec(N_QHEADS), head_spec(N_KHEADS), head_spec(N_QHEADS),
                 pl.BlockSpec((None, TM, CONV_CH), tok)]
    out_shape = [jax.ShapeDtypeStruct((B, N_QHEADS, S, LANES), BF16),
                 jax.ShapeDtypeStruct((B, N_KHEADS, S, LANES), BF16),
                 jax.ShapeDtypeStruct((B, N_QHEADS, S, LANES), BF16),
                 jax.ShapeDtypeStruct((B, S, CONV_CH), F32)]
    if has_moe:
        out_specs = [pl.BlockSpec((None, TM, D), tok)] + out_specs
        out_shape = [jax.ShapeDtypeStruct((B, S, D), F32)] + out_shape
    outs = pl.pallas_call(
        functools.partial(_proj_kernel, has_moe=has_moe),
        grid=(B, nt), in_specs=in_specs, out_specs=out_specs, out_shape=out_shape,
        compiler_params=_cparams(("parallel", "parallel")),
        name="project",
    )(*args)
    if has_moe:
        return outs[0], outs[1:]
    return xs, outs


def _scores(q, k):
    return lax.dot_general(q, k, (((1,), (1,)), ((), ())), preferred_element_type=F32)


def _softmax_update(s, v, m, acc, c):
    m_new = jnp.maximum(m, jnp.max(s, axis=-1, keepdims=True))
    alpha = jnp.exp2((m - m_new) * c)
    p = jnp.exp2((s - m_new) * c).astype(BF16)
    acc = alpha * acc + jnp.dot(p, v, preferred_element_type=F32)
    return m_new, acc


def _attn_kernel(q_ref, k0_ref, k1_ref, v_ref, o_ref, s_ref, *, tile_off, n_ctx, n_all, tk):
    hp = pl.program_id(1)
    j = pl.program_id(2) + tile_off
    c = jnp.where(hp < MLA_HEADS // 2, MLA_SCALE * LOG2E, GQA_SCALE * LOG2E).astype(F32)
    q0, q1 = q_ref[0], q_ref[1]
    init = (jnp.full((TQ, 1), -jnp.inf, F32), jnp.zeros((TQ, LANES), F32))

    def finish(acc0, acc1):
        lane = lax.broadcasted_iota(jnp.int32, (TQ, LANES), 1)
        l0 = acc0[:, ONES_LANE[0]:ONES_LANE[0] + 1]
        l1 = acc1[:, ONES_LANE[1]:ONES_LANE[1] + 1]
        o_ref[...] = jnp.where(lane < LANES // 2, acc0 / l0, acc1 / l1)

    @pl.when(j == 0)
    def _():
        _, acc0 = _softmax_update(_scores(q0, k0_ref[0:n_ctx, :]), v_ref[0, 0:n_ctx, :], *init, c)
        _, acc1 = _softmax_update(_scores(q1, k1_ref[0:n_ctx, :]), v_ref[1, 0:n_ctx, :], *init, c)
        finish(acc0, acc1)

    @pl.when(j > 0)
    def _():
        n = n_all // tk

        def put(slot, r):
            s_ref[slot, 0] = _scores(q0, k0_ref[pl.ds(r, tk), :])
            s_ref[slot, 1] = _scores(q1, k1_ref[pl.ds(r, tk), :])

        def step(slot, r, carry):
            m0, a0, m1, a1 = carry
            m0, a0 = _softmax_update(s_ref[slot, 0], v_ref[0, pl.ds(r, tk), :], m0, a0, c)
            m1, a1 = _softmax_update(s_ref[slot, 1], v_ref[1, pl.ds(r, tk), :], m1, a1, c)
            return m0, a0, m1, a1

        def body(t, carry):
            r0 = pl.multiple_of(2 * t * tk, tk)
            r1 = pl.multiple_of(r0 + tk, tk)
            r2 = pl.multiple_of(r1 + tk, tk)
            put(1, r1)
            carry = step(0, r0, carry)
            put(0, r2)
            return step(1, r1, carry)

        put(0, 0)
        paired = (n - 1) // 2
        carry = lax.fori_loop(0, paired, body, init + init)
        for t in range(2 * paired, n):
            if t + 1 < n:
                put((t + 1) % 2, (t + 1) * tk)
            carry = step(t % 2, t * tk, carry)
        finish(carry[1], carry[3])


def _kv_head(h):
    group = GQA_HEADS // GQA_KV_HEADS
    return jnp.where(h < MLA_HEADS, h, MLA_HEADS + (h - MLA_HEADS) // group)


def _attention(q, k, v, n_ctx, tile_off):
    B, _, S, _ = q.shape
    nq = S // TQ - tile_off
    tk = next(t for t in range(TK_MAX, 0, -LANES) if S % t == 0)
    kspec = lambda i: pl.BlockSpec((None, None, S, LANES), lambda b, hp, j: (b, _kv_head(2 * hp + i), 0, 0))
    return pl.pallas_call(
        functools.partial(_attn_kernel, tile_off=tile_off, n_ctx=n_ctx, n_all=S, tk=tk),
        grid=(B, N_PAIRS, nq),
        in_specs=[pl.BlockSpec((None, 2, TQ, LANES), lambda b, hp, j: (b, hp, j + tile_off, 0)),
                  kspec(0), kspec(1),
                  pl.BlockSpec((None, 2, S, LANES), lambda b, hp, j: (b, hp, 0, 0))],
        out_specs=pl.BlockSpec((None, TQ, LANES), lambda b, hp, j: (b, j, hp)),
        out_shape=jax.ShapeDtypeStruct((B, nq * TQ, N_PAIRS * LANES), F32),
        scratch_shapes=[pltpu.VMEM((2, 2, TQ, tk), F32)],
        compiler_params=_cparams(("parallel", "parallel", "arbitrary")),
        name="attention",
    )(q, k, k, v)


def _mix_kernel(o_ref, up_ref, uc_ref, un_ref, x_ref, mod_ref, cw_ref, cb_ref, lg_ref, lb_ref,
                mg_ref, wout_ref, n2_ref, rw_ref,
                xo_ref, h_ref, aff_ref, ext_ref, *, tile_off, last_tile):
    j = pl.program_id(1) + tile_off
    has_prev = j >= 2
    has_next = (j >= 1) & (j < last_tile)
    ext_ref[0:HALO, :] = jnp.where(has_prev, up_ref[TM - HALO:TM, :], 0.0)
    ext_ref[HALO:HALO + TM, :] = uc_ref[...]
    ext_ref[HALO + TM:HALO + TM + HALO, :] = jnp.where(has_next, un_ref[0:HALO, :], 0.0)
    y = jnp.zeros((TM, CONV_CH), F32) + cb_ref[...]
    for t in range(CONV_WIDTH):
        r = HALO - CONV_WIDTH // 2 + t
        y = y + ext_ref[r:r + TM, :] * cw_ref[t:t + 1, :]
    mu = jnp.mean(y, axis=-1, keepdims=True)
    yc = y - mu
    yn = yc * lax.rsqrt(jnp.mean(yc * yc, axis=-1, keepdims=True) + EPS) * lg_ref[...] + lb_ref[...]
    o_conv = yn * jax.nn.sigmoid(yn)

    o = o_ref[...]
    w_mla = MLA_HEADS * MLA_V
    w_att = w_mla + GQA_HEADS * GQA_HEAD_DIM
    cat = jnp.concatenate([
        _rms(o[:, :w_mla], mg_ref[:, :w_mla]).astype(BF16),
        _rms(o[:, w_mla:w_att], mg_ref[:, w_mla:w_att]).astype(BF16),
        _rms(o_conv, mg_ref[:, w_att:]).astype(BF16)], axis=-1)
    mix = jnp.dot(cat, wout_ref[...], preferred_element_type=F32)
    x = x_ref[...] + mod_ref[2:3, :] * mix
    xo_ref[...] = x
    hm = _rms(x, n2_ref[...]) * (1.0 + mod_ref[4:5, :]) + mod_ref[3:4, :]
    h_ref[...] = hm
    logits = lax.dot_general(rw_ref[...], hm, (((1,), (1,)), ((), ())),
                             preferred_element_type=F32, precision=HIGHEST)
    e = jnp.exp(logits - jnp.max(logits, axis=0, keepdims=True))
    aff_ref[...] = e / jnp.sum(e, axis=0, keepdims=True)


def _mix(o, u, xs, mod, lw, tile_off):
    B, S_o, _ = o.shape
    D = xs.shape[-1]
    nt_all = u.shape[1] // TM
    nt = S_o // TM
    const2 = lambda b, j: (0, 0)
    weights = [lw["conv_w"], lw["conv_dw_b"], lw["conv_ln_g"], lw["conv_ln_b"], lw["mix_out_g"], lw["w_out"],
               lw["norm2_g"], lw["router_wt"]]
    uspec = lambda d: pl.BlockSpec(
        (None, TM, CONV_CH), lambda b, j: (b, jnp.clip(j + tile_off + d, 0, nt_all - 1), 0))
    return pl.pallas_call(
        functools.partial(_mix_kernel, tile_off=tile_off, last_tile=nt_all - 1),
        grid=(B, nt),
        in_specs=[pl.BlockSpec((None, TM, o.shape[-1]), lambda b, j: (b, j, 0)),
                  uspec(-1), uspec(0), uspec(1),
                  pl.BlockSpec((None, TM, D), lambda b, j: (b, j + tile_off, 0)),
                  pl.BlockSpec((None, 6, D), lambda b, j: (_mod_row(b, j + tile_off), 0, 0))]
                 + [pl.BlockSpec(w.shape, const2) for w in weights],
        out_specs=[pl.BlockSpec((None, TM, D), lambda b, j: (b, j, 0)),
                   pl.BlockSpec((None, TM, D), lambda b, j: (b, j, 0)),
                   pl.BlockSpec((None, N_EXPERTS, TM), lambda b, j: (b, 0, j))],
        out_shape=[jax.ShapeDtypeStruct((B, S_o, D), F32),
                   jax.ShapeDtypeStruct((B, S_o, D), F32),
                   jax.ShapeDtypeStruct((B, N_EXPERTS, S_o), F32)],
        scratch_shapes=[pltpu.VMEM((TM + 2 * HALO, CONV_CH), F32)],
        compiler_params=_cparams(("parallel", "parallel")),
        name="mix",
    )(o, u, u, u, xs, mod, *weights)


def _moe_kernel(idx_ref, h_hbm, g_ref, w1_ref, w3_ref, w2_ref, y_ref, xbuf0, xbuf1, w1b, w3b, w2b, sem,
                *, nchunk, rows, per_it, r_pad):
    e, f = pl.program_id(0), pl.program_id(1)
    ne, nf = pl.num_programs(0), pl.num_programs(1)
    xbufs = (xbuf0, xbuf1)

    def request(expert, r0, n, into):
        for k in range(n):
            src = idx_ref[expert * r_pad + r0 + k]
            pltpu.make_async_copy(h_hbm.at[pl.ds(src, 1)], xbufs[into].at[pl.ds(r0 + k, 1)], sem.at[into]).start()

    def wait_rows(which):
        pltpu.make_async_copy(h_hbm.at[pl.ds(0, r_pad)], xbufs[which], sem.at[which]).wait()

    @pl.when((e == 0) & (f == 0))
    def _():
        def first(i, carry):
            request(0, pl.multiple_of(i * per_it, 8), per_it, 0)
            return carry
        lax.fori_loop(0, r_pad // per_it, first, 0)

    w1b[...] = w1_ref[...].astype(BF16)
    w3b[...] = w3_ref[...].astype(BF16)
    w2b[...] = w2_ref[...].astype(BF16)
    nxt = jnp.minimum(e + 1, ne - 1)

    def run(cur):
        @pl.when(f == 0)
        def _():
            wait_rows(cur)

        def body(c, carry):
            request(nxt, pl.multiple_of((f * nchunk + c) * per_it, 8), per_it, 1 - cur)
            r0 = pl.multiple_of(c * rows, rows)
            xb = xbufs[cur][pl.ds(r0, rows), :].astype(BF16)
            h1 = jnp.dot(xb, w1b[...], preferred_element_type=F32)
            h3 = jnp.dot(xb, w3b[...], preferred_element_type=F32)
            hid = (h1 * jax.nn.sigmoid(h1) * h3).astype(BF16)
            part = jnp.dot(hid, w2b[...], preferred_element_type=F32)

            @pl.when(f == 0)
            def _():
                y_ref[pl.ds(r0, rows), :] = part

            @pl.when(f > 0)
            def _():
                y_ref[pl.ds(r0, rows), :] += part
            return carry

        lax.fori_loop(0, nchunk, body, 0)

        @pl.when((e == ne - 1) & (f == nf - 1))
        def _():
            wait_rows(1 - cur)

    for parity in (0, 1):
        pl.when(e % 2 == parity)(functools.partial(run, parity))

    @pl.when(f == nf - 1)
    def _():
        y_ref[...] = y_ref[...] * g_ref[...]


def _moe_tiles(R, FF):
    rows = next(r for r in range(512, 15, -16) if R % r == 0)
    tf = next(t for t in range(512, LANES - 1, -LANES) if FF % t == 0)
    steps = (R // rows) * (FF // tf)
    per_it = 8 * (-(-R // (8 * steps)))
    return rows, tf, per_it, per_it * steps


def _experts(h_flat, gidx, gate, w1, w3, w2, l):
    E, R = gidx.shape
    D = h_flat.shape[-1]
    FF = w1.shape[-1]
    rows, tf, per_it, r_pad = _moe_tiles(R, FF)
    idx = jnp.pad(gidx, ((0, 0), (0, r_pad - R))).reshape(E * r_pad)
    grid_spec = pltpu.PrefetchScalarGridSpec(
        num_scalar_prefetch=1,
        grid=(E, FF // tf),
        in_specs=[pl.BlockSpec(memory_space=pl.ANY),
                  pl.BlockSpec((None, R, 1), lambda e, f, idx: (e, 0, 0)),
                  pl.BlockSpec((None, None, D, tf), lambda e, f, idx: (l, e, 0, f)),
                  pl.BlockSpec((None, None, D, tf), lambda e, f, idx: (l, e, 0, f)),
                  pl.BlockSpec((None, None, tf, D), lambda e, f, idx: (l, e, f, 0))],
        out_specs=pl.BlockSpec((None, R, D), lambda e, f, idx: (e, 0, 0)),
        scratch_shapes=[pltpu.VMEM((r_pad, D), F32), pltpu.VMEM((r_pad, D), F32),
                        pltpu.VMEM((D, tf), BF16), pltpu.VMEM((D, tf), BF16), pltpu.VMEM((tf, D), BF16),
                        pltpu.SemaphoreType.DMA((2,))])
    return pl.pallas_call(
        functools.partial(_moe_kernel, nchunk=R // rows, rows=rows, per_it=per_it, r_pad=r_pad),
        grid_spec=grid_spec,
        out_shape=jax.ShapeDtypeStruct((E, R, D), F32),
        compiler_params=_cparams(("arbitrary", "arbitrary")),
        name="experts",
    )(idx, h_flat, gate, w1, w3, w2)


def _final_kernel(x_ref, moe_ref, mod_ref, g_ref, o_ref):
    x = x_ref[...] + mod_ref[5:6, :] * moe_ref[...]
    o_ref[...] = _rms(x, g_ref[...])


def _final(x, moe, mod, g):
    B, N, D = x.shape
    tok = lambda b, j: (b, j, 0)
    return pl.pallas_call(
        _final_kernel,
        grid=(B, N // TM),
        in_specs=[pl.BlockSpec((None, TM, D), tok), pl.BlockSpec((None, TM, D), tok),
                  pl.BlockSpec((None, 6, D), lambda b, j: (b, 0, 0)),
                  pl.BlockSpec((1, D), lambda b, j: (0, 0))],
        out_specs=pl.BlockSpec((None, TM, D), tok),
        out_shape=jax.ShapeDtypeStruct((B, N, D), F32),
        compiler_params=_cparams(("parallel", "parallel")),
        name="final_norm",
    )(x, moe, mod, g)


def _pad_cols(w, lo, width=LANES):
    return jnp.pad(w, ((0, 0), (lo, width - lo - w.shape[1])))


def _prep_layer(p):
    w_in = p["w_in"]
    o = 0
    parts = {}
    for name, size in (("cq", MLA_Q_RANK), ("ckv", MLA_KV_RANK), ("kr", MLA_ROPE),
                       ("zq", GQA_HEADS * GQA_HEAD_DIM), ("zk", GQA_KV_HEADS * GQA_HEAD_DIM),
                       ("zv", GQA_KV_HEADS * GQA_HEAD_DIM), ("zc", 2 * CONV_CH)):
        parts[name] = w_in[:, o:o + size]
        o += size
    hd = GQA_HEAD_DIM
    group = GQA_HEADS // GQA_KV_HEADS
    cols = [parts["cq"], parts["ckv"]]
    cols += [_pad_cols(parts["zq"][:, h * hd:(h + 1) * hd], 0) for h in range(GQA_HEADS)]
    cols += [_pad_cols(parts["zk"][:, h * hd:(h + 1) * hd], 0) for h in range(GQA_KV_HEADS)]
    cols += [_pad_cols(parts["zv"][:, (h // group) * hd:(h // group + 1) * hd], (h % 2) * hd)
             for h in range(GQA_HEADS)]
    cols += [parts["zc"], _pad_cols(parts["kr"], MLA_NOPE)]
    w_in_p = jnp.concatenate(cols, axis=1).astype(BF16)

    dq = MLA_NOPE + MLA_ROPE
    w_uq = jnp.concatenate([_pad_cols(p["mla_w_uq"][:, h * dq:(h + 1) * dq], 0) for h in range(MLA_HEADS)],
                           axis=1).astype(BF16)
    dkv = MLA_NOPE + MLA_V
    wk = [_pad_cols(p["mla_w_ukv"][:, h * dkv:h * dkv + MLA_NOPE], 0) for h in range(MLA_HEADS)]
    wv = [_pad_cols(p["mla_w_ukv"][:, h * dkv + MLA_NOPE:(h + 1) * dkv], (h % 2) * MLA_V) for h in range(MLA_HEADS)]
    w_ukv = jnp.concatenate(wk + wv, axis=1).astype(BF16)

    row = lambda a: a.reshape(1, -1)
    return {
        "norm1_g": row(p["norm1_g"]), "w_in": w_in_p,
        "mla_q_norm_g": row(p["mla_q_norm_g"]), "w_uq": w_uq,
        "mla_kv_norm_g": row(p["mla_kv_norm_g"]), "w_ukv": w_ukv,
        "gqa_q_norm_g": _pad_cols(row(p["gqa_q_norm_g"]), 0), "gqa_k_norm_g": _pad_cols(row(p["gqa_k_norm_g"]), 0),
        "conv_w": jnp.pad(p["conv_dw_w"], ((0, 32 - CONV_WIDTH), (0, 0))),
        "conv_dw_b": row(p["conv_dw_b"]), "conv_ln_g": row(p["conv_ln_g"]), "conv_ln_b": row(p["conv_ln_b"]),
        "mix_out_g": row(p["mix_out_g"]), "w_out": p["w_out"].astype(BF16),
        "norm2_g": row(p["norm2_g"]), "router_wt": p["router_w"].T,
    }


def _rope_tables(n, n_ctx):
    rows = n // GRID_W
    row = jnp.repeat(jnp.arange(rows), GRID_W).astype(F32)
    col = jnp.tile(jnp.arange(GRID_W), rows).astype(F32)
    tabs = []
    for d_rot, lo in ((MLA_ROPE, MLA_NOPE), (GQA_HEAD_DIM, 0)):
        d_axis = d_rot // 2
        inv = ROPE_BASE ** (-jnp.arange(0, d_axis, 2, dtype=F32) / d_axis)
        ang = jnp.concatenate([row[:, None] * inv, col[:, None] * inv], axis=-1)
        cos, sin = jnp.cos(ang), jnp.sin(ang)
        cos_b = jnp.pad(jnp.concatenate([cos, cos], axis=1) - 1.0, ((n_ctx, 0), (lo, LANES - lo - d_rot))) + 1.0
        sin_b = jnp.pad(jnp.concatenate([-sin, sin], axis=1), ((n_ctx, 0), (lo, LANES - lo - d_rot)))
        tabs += [cos_b, sin_b]
    return tabs


def _route(aff, stride, cap, row0):
    gate, idx = lax.top_k(aff, cap)
    B = aff.shape[0]
    gidx = idx + (row0 + jnp.arange(B, dtype=idx.dtype) * stride)[:, None, None]
    E = aff.shape[1]
    return gidx.transpose(1, 0, 2).reshape(E, B * cap), gate.transpose(1, 0, 2).reshape(E, B * cap)


def _moe_layer(aff, h, n_ctx, with_ctx, w1, w3, w2, l):
    B, S_o, D = h.shape
    lat0 = n_ctx if with_ctx else 0
    n = S_o - lat0
    sets = [_route(aff[:, :, lat0:], S_o, CAPACITY_FACTOR * n // N_EXPERTS, lat0)]
    if with_ctx:
        sets.append(_route(aff[:, :, :n_ctx], S_o, CAPACITY_FACTOR * n_ctx // N_EXPERTS, 0))
    gidx = jnp.concatenate([s[0] for s in sets], axis=1)
    gate = jnp.concatenate([s[1] for s in sets], axis=1)
    y = _experts(h.reshape(B * S_o, D), gidx, gate[..., None], w1, w3, w2, l)
    out = jnp.zeros((B * S_o, D), F32).at[gidx.reshape(-1)].add(y.reshape(-1, D))
    return out.reshape(B, S_o, D)


def kernel(x, c, ctx, c_ctx, mod_w, mod_b, norm1_g, w_in, mla_q_norm_g, mla_w_uq, mla_kv_norm_g, mla_w_ukv,
           gqa_q_norm_g, gqa_k_norm_g, conv_dw_w, conv_dw_b, conv_ln_g, conv_ln_b, mix_out_g, w_out, norm2_g,
           router_w, exp_w1, exp_w3, exp_w2, final_g):
    B, N, D = x.shape
    n_ctx = ctx.shape[1]
    depth = mod_w.shape[0]
    assert n_ctx == TM == TQ and N % TM == 0
    stacked = {"norm1_g": norm1_g, "w_in": w_in, "mla_q_norm_g": mla_q_norm_g, "mla_w_uq": mla_w_uq,
               "mla_kv_norm_g": mla_kv_norm_g, "mla_w_ukv": mla_w_ukv, "gqa_q_norm_g": gqa_q_norm_g,
               "gqa_k_norm_g": gqa_k_norm_g, "conv_dw_w": conv_dw_w, "conv_dw_b": conv_dw_b,
               "conv_ln_g": conv_ln_g, "conv_ln_b": conv_ln_b, "mix_out_g": mix_out_g, "w_out": w_out,
               "norm2_g": norm2_g, "router_w": router_w}

    assert B <= CTX_ROW
    cvec = jnp.zeros((MOD_ROWS, D), F32).at[:B].set(c).at[CTX_ROW].set(c_ctx)
    mods = _modulation(cvec, mod_w, mod_b).reshape(depth, MOD_ROWS, 6, D)
    tabs = _rope_tables(N, n_ctx)

    xs = jnp.concatenate([ctx, x], axis=1)
    moe, modp = None, None
    for l in range(depth):
        lw = _prep_layer({k: v[l] for k, v in stacked.items()})
        last = l == depth - 1
        tile_off = 1 if last else 0
        xs, (q, k, v, u) = _project(xs, moe, modp, mods[l], lw, tabs)
        o = _attention(q, k, v, n_ctx, tile_off)
        xs, h, aff = _mix(o, u, xs, mods[l], lw, tile_off)
        moe = _moe_layer(aff, h, n_ctx, not last, exp_w1, exp_w3, exp_w2, l)
        modp = mods[l]
    return _final(xs, moe, modp, final_g.reshape(1, D))
```

```python
import functools
import math

import jax
import jax.numpy as jnp
from jax import lax
from jax.experimental import pallas as pl
from jax.experimental.pallas import tpu as pltpu

F32 = jnp.float32
BF16 = jnp.bfloat16
HIGHEST = lax.Precision.HIGHEST

GRID_W = 64
ROPE_BASE = 10000.0
EPS = 1e-6

MLA_HEADS = 8
MLA_Q_RANK = 256
MLA_KV_RANK = 128
MLA_NOPE = 64
MLA_ROPE = 32
MLA_V = 64
GQA_HEADS = 4
GQA_KV_HEADS = 2
GQA_HEAD_DIM = 64
CONV_CH = 256
CONV_WIDTH = 31
N_EXPERTS = 16
CAPACITY_FACTOR = 2

MLA_SCALE = 1.0 / math.sqrt(MLA_NOPE + MLA_ROPE)
GQA_SCALE = 1.0 / math.sqrt(GQA_HEAD_DIM)

LANES = 128
N_QHEADS = MLA_HEADS + GQA_HEADS
N_KHEADS = MLA_HEADS + GQA_KV_HEADS
N_PAIRS = N_QHEADS // 2
TM = 256
TQ = 256
TK_MAX = 2816
ONES_LANE = (MLA_V, 0)
LOG2E = math.log2(math.e)
HALO = 16
VMEM_LIMIT = 56 * 1024 * 1024
MOD_ROWS = 8
CTX_ROW = MOD_ROWS - 1

C_CQ = 0
C_CKV = C_CQ + MLA_Q_RANK
C_ZQ = C_CKV + MLA_KV_RANK
C_ZK = C_ZQ + GQA_HEADS * LANES
C_ZV = C_ZK + GQA_KV_HEADS * LANES
C_ZA = C_ZV + GQA_HEADS * LANES
C_ZB = C_ZA + CONV_CH
C_KR = C_ZB + CONV_CH
IN_PAD = C_KR + LANES


def _cparams(sem):
    return pltpu.CompilerParams(dimension_semantics=sem, vmem_limit_bytes=VMEM_LIMIT)


def _rms(x, g):
    return x * lax.rsqrt(jnp.mean(x * x, axis=-1, keepdims=True) + EPS) * g


def _head_rms(x, g, width):
    return x * lax.rsqrt(jnp.sum(x * x, axis=-1, keepdims=True) * (1.0 / width) + EPS) * g


def _rope(x, cos, sin, lo, half):
    lane = lax.broadcasted_iota(jnp.int32, x.shape, 1)
    first = (lane >= lo) & (lane < lo + half)
    rot = jnp.where(first, pltpu.roll(x, LANES - half, 1), pltpu.roll(x, half, 1))
    return x * cos + rot * sin


def _ones_col(h):
    lane = lax.broadcasted_iota(jnp.int32, (1, LANES), 1)
    return (lane == ONES_LANE[h % 2]).astype(F32)


def _mod_kernel(c_ref, w_ref, b_ref, o_ref):
    c = c_ref[...]
    a = c * jax.nn.sigmoid(c)
    o_ref[...] = jnp.dot(a, w_ref[...], preferred_element_type=F32, precision=HIGHEST) + b_ref[...]


def _modulation(cvec, mod_w, mod_b):
    L, D, W = mod_w.shape
    tn = 1536
    return pl.pallas_call(
        _mod_kernel,
        grid=(L, W // tn),
        in_specs=[pl.BlockSpec((MOD_ROWS, D), lambda l, n: (0, 0)),
                  pl.BlockSpec((None, D, tn), lambda l, n: (l, 0, n)),
                  pl.BlockSpec((None, 1, tn), lambda l, n: (l, 0, n))],
        out_specs=pl.BlockSpec((None, MOD_ROWS, tn), lambda l, n: (l, 0, n)),
        out_shape=jax.ShapeDtypeStruct((L, MOD_ROWS, W), F32),
        compiler_params=_cparams(("arbitrary", "arbitrary")),
        name="modulation",
    )(cvec, mod_w, mod_b.reshape(L, 1, W))


def _proj_kernel(*refs, has_moe):
    if has_moe:
        x_ref, moe_ref, modp_ref, refs = refs[0], refs[1], refs[2], refs[3:]
    else:
        x_ref, refs = refs[0], refs[1:]
    (mod_ref, n1_ref, win_ref, gq_ref, wuq_ref, gkv_ref, wukv_ref, ggq_ref, ggk_ref,
     cm_ref, sm_ref, cg_ref, sg_ref) = refs[:13]
    outs = refs[13:]
    if has_moe:
        xo_ref, outs = outs[0], outs[1:]
    q_ref, k_ref, v_ref, u_ref = outs

    x = x_ref[...]
    if has_moe:
        x = x + modp_ref[5:6, :] * moe_ref[...]
        xo_ref[...] = x
    hm = _rms(x, n1_ref[...]) * (1.0 + mod_ref[1:2, :]) + mod_ref[0:1, :]
    z = jnp.dot(hm.astype(BF16), win_ref[...], preferred_element_type=F32)

    cos_m, sin_m, cos_g, sin_g = cm_ref[...], sm_ref[...], cg_ref[...], sg_ref[...]

    cqn = _rms(z[:, C_CQ:C_CQ + MLA_Q_RANK], gq_ref[...])
    q = jnp.dot(cqn.astype(BF16), wuq_ref[...], preferred_element_type=F32)
    for h in range(MLA_HEADS):
        q_ref[h] = _rope(q[:, h * LANES:(h + 1) * LANES], cos_m, sin_m, MLA_NOPE, MLA_ROPE // 2).astype(BF16)

    ckvn = _rms(z[:, C_CKV:C_CKV + MLA_KV_RANK], gkv_ref[...])
    kv = jnp.dot(ckvn.astype(BF16), wukv_ref[...], preferred_element_type=F32)
    kr = _rope(z[:, C_KR:C_KR + LANES], cos_m, sin_m, MLA_NOPE, MLA_ROPE // 2)
    for h in range(MLA_HEADS):
        k_ref[h] = (kv[:, h * LANES:(h + 1) * LANES] + kr).astype(BF16)
        v_ref[h] = (kv[:, (MLA_HEADS + h) * LANES:(MLA_HEADS + h + 1) * LANES] + _ones_col(h)).astype(BF16)

    for h in range(GQA_HEADS):
        zq = z[:, C_ZQ + h * LANES:C_ZQ + (h + 1) * LANES]
        q_ref[MLA_HEADS + h] = _rope(_head_rms(zq, ggq_ref[...], GQA_HEAD_DIM), cos_g, sin_g,
                                     0, GQA_HEAD_DIM // 2).astype(BF16)
        v_ref[MLA_HEADS + h] = (z[:, C_ZV + h * LANES:C_ZV + (h + 1) * LANES] + _ones_col(h)).astype(BF16)
    for h in range(GQA_KV_HEADS):
        zk = z[:, C_ZK + h * LANES:C_ZK + (h + 1) * LANES]
        k_ref[MLA_HEADS + h] = _rope(_head_rms(zk, ggk_ref[...], GQA_HEAD_DIM), cos_g, sin_g,
                                     0, GQA_HEAD_DIM // 2).astype(BF16)

    u_ref[...] = z[:, C_ZA:C_ZA + CONV_CH] * jax.nn.sigmoid(z[:, C_ZB:C_ZB + CONV_CH])


def _mod_row(b, j):
    return jnp.where(j == 0, CTX_ROW, b)


def _project(xs, moe, modp, mod, lw, tabs):
    B, S, D = xs.shape
    nt = S // TM
    has_moe = moe is not None
    tok = lambda b, j: (b, j, 0)
    const2 = lambda b, j: (0, 0)
    modspec = pl.BlockSpec((None, 6, D), lambda b, j: (_mod_row(b, j), 0, 0))
    in_specs = [pl.BlockSpec((None, TM, D), tok)]
    args = [xs]
    if has_moe:
        in_specs += [pl.BlockSpec((None, TM, D), tok), modspec]
        args += [moe, modp]
    weights = [lw["norm1_g"], lw["w_in"], lw["mla_q_norm_g"], lw["w_uq"], lw["mla_kv_norm_g"], lw["w_ukv"],
               lw["gqa_q_norm_g"], lw["gqa_k_norm_g"]]
    in_specs += [modspec] + [pl.BlockSpec(w.shape, const2) for w in weights]
    in_specs += [pl.BlockSpec((TM, LANES), lambda b, j: (j, 0))] * 4
    args += [mod] + weights + list(tabs)
    head_spec = lambda n: pl.BlockSpec((None, n, TM, LANES), lambda b, j: (b, 0, j, 0))
    out_specs = [head_spec(N_QHEADS), head_spec(N_KHEADS), head_spec(N_QHEADS),
                 pl.BlockSpec((None, TM, CONV_CH), tok)]
    out_shape = [jax.ShapeDtypeStruct((B, N_QHEADS, S, LANES), BF16),
                 jax.ShapeDtypeStruct((B, N_KHEADS, S, LANES), BF16),
                 jax.ShapeDtypeStruct((B, N_QHEADS, S, LANES), BF16),
                 jax.ShapeDtypeStruct((B, S, CONV_CH), F32)]
    if has_moe:
        out_specs = [pl.BlockSpec((None, TM, D), tok)] + out_specs
        out_shape = [jax.ShapeDtypeStruct((B, S, D), F32)] + out_shape
    outs = pl.pallas_call(
        functools.partial(_proj_kernel, has_moe=has_moe),
        grid=(B, nt), in_specs=in_specs, out_specs=out_specs, out_shape=out_shape,
        compiler_params=_cparams(("parallel", "parallel")),
        name="project",
    )(*args)
    if has_moe:
        return outs[0], outs[1:]
    return xs, outs


def _scores(q, k):
    return lax.dot_general(q, k, (((1,), (1,)), ((), ())), preferred_element_type=F32)


def _softmax_update(s, v, m, acc, c):
    m_new = jnp.maximum(m, jnp.max(s, axis=-1, keepdims=True))
    alpha = jnp.exp2((m - m_new) * c)
    p = jnp.exp2((s - m_new) * c).astype(BF16)
    acc = alpha * acc + jnp.dot(p, v, preferred_element_type=F32)
    return m_new, acc


def _attn_kernel(q_ref, k0_ref, k1_ref, v_ref, o_ref, s_ref, *, tile_off, n_ctx, n_all, tk):
    hp = pl.program_id(1)
    j = pl.program_id(2) + tile_off
    c = jnp.where(hp < MLA_HEADS // 2, MLA_SCALE * LOG2E, GQA_SCALE * LOG2E).astype(F32)
    q0, q1 = q_ref[0], q_ref[1]
    init = (jnp.full((TQ, 1), -jnp.inf, F32), jnp.zeros((TQ, LANES), F32))

    def finish(acc0, acc1):
        lane = lax.broadcasted_iota(jnp.int32, (TQ, LANES), 1)
        l0 = acc0[:, ONES_LANE[0]:ONES_LANE[0] + 1]
        l1 = acc1[:, ONES_LANE[1]:ONES_LANE[1] + 1]
        o_ref[...] = jnp.where(lane < LANES // 2, acc0 / l0, acc1 / l1)

    @pl.when(j == 0)
    def _():
        _, acc0 = _softmax_update(_scores(q0, k0_ref[0:n_ctx, :]), v_ref[0, 0:n_ctx, :], *init, c)
        _, acc1 = _softmax_update(_scores(q1, k1_ref[0:n_ctx, :]), v_ref[1, 0:n_ctx, :], *init, c)
        finish(acc0, acc1)

    @pl.when(j > 0)
    def _():
        n = n_all // tk

        def put(slot, r):
            s_ref[slot, 0] = _scores(q0, k0_ref[pl.ds(r, tk), :])
            s_ref[slot, 1] = _scores(q1, k1_ref[pl.ds(r, tk), :])

        def step(slot, r, carry):
            m0, a0, m1, a1 = carry
            m0, a0 = _softmax_update(s_ref[slot, 0], v_ref[0, pl.ds(r, tk), :], m0, a0, c)
            m1, a1 = _softmax_update(s_ref[slot, 1], v_ref[1, pl.ds(r, tk), :], m1, a1, c)
            return m0, a0, m1, a1

        def body(t, carry):
            r0 = pl.multiple_of(2 * t * tk, tk)
            r1 = pl.multiple_of(r0 + tk, tk)
            r2 = pl.multiple_of(r1 + tk, tk)
            put(1, r1)
            carry = step(0, r0, carry)
            put(0, r2)
            return step(1, r1, carry)

        put(0, 0)
        paired = (n - 1) // 2
        carry = lax.fori_loop(0, paired, body, init + init)
        for t in range(2 * paired, n):
            if t + 1 < n:
                put((t + 1) % 2, (t + 1) * tk)
            carry = step(t % 2, t * tk, carry)
        finish(carry[1], carry[3])


def _kv_head(h):
    group = GQA_HEADS // GQA_KV_HEADS
    return jnp.where(h < MLA_HEADS, h, MLA_HEADS + (h - MLA_HEADS) // group)


def _attention(q, k, v, n_ctx, tile_off):
    B, _, S, _ = q.shape
    nq = S // TQ - tile_off
    tk = next(t for t in range(TK_MAX, 0, -LANES) if S % t == 0)
    kspec = lambda i: pl.BlockSpec((None, None, S, LANES), lambda b, hp, j: (b, _kv_head(2 * hp + i), 0, 0))
    return pl.pallas_call(
        functools.partial(_attn_kernel, tile_off=tile_off, n_ctx=n_ctx, n_all=S, tk=tk),
        grid=(B, N_PAIRS, nq),
        in_specs=[pl.BlockSpec((None, 2, TQ, LANES), lambda b, hp, j: (b, hp, j + tile_off, 0)),
                  kspec(0), kspec(1),
                  pl.BlockSpec((None, 2, S, LANES), lambda b, hp, j: (b, hp, 0, 0))],
        out_specs=pl.BlockSpec((None, TQ, LANES), lambda b, hp, j: (b, j, hp)),
        out_shape=jax.ShapeDtypeStruct((B, nq * TQ, N_PAIRS * LANES), F32),
        scratch_shapes=[pltpu.VMEM((2, 2, TQ, tk), F32)],
        compiler_params=_cparams(("parallel", "parallel", "arbitrary")),
        name="attention",
    )(q, k, k, v)


def _mix_kernel(o_ref, up_ref, uc_ref, un_ref, x_ref, mod_ref, cw_ref, cb_ref, lg_ref, lb_ref,
                mg_ref, wout_ref, n2_ref, rw_ref,
                xo_ref, h_ref, aff_ref, ext_ref, *, tile_off, last_tile):
    j = pl.program_id(1) + tile_off
    has_prev = j >= 2
    has_next = (j >= 1) & (j < last_tile)
    ext_ref[0:HALO, :] = jnp.where(has_prev, up_ref[TM - HALO:TM, :], 0.0)
    ext_ref[HALO:HALO + TM, :] = uc_ref[...]
    ext_ref[HALO + TM:HALO + TM + HALO, :] = jnp.where(has_next, un_ref[0:HALO, :], 0.0)
    y = jnp.zeros((TM, CONV_CH), F32) + cb_ref[...]
    for t in range(CONV_WIDTH):
        r = HALO - CONV_WIDTH // 2 + t
        y = y + ext_ref[r:r + TM, :] * cw_ref[t:t + 1, :]
    mu = jnp.mean(y, axis=-1, keepdims=True)
    yc = y - mu
    yn = yc * lax.rsqrt(jnp.mean(yc * yc, axis=-1, keepdims=True) + EPS) * lg_ref[...] + lb_ref[...]
    o_conv = yn * jax.nn.sigmoid(yn)

    o = o_ref[...]
    w_mla = MLA_HEADS * MLA_V
    w_att = w_mla + GQA_HEADS * GQA_HEAD_DIM
    cat = jnp.concatenate([
        _rms(o[:, :w_mla], mg_ref[:, :w_mla]).astype(BF16),
        _rms(o[:, w_mla:w_att], mg_ref[:, w_mla:w_att]).astype(BF16),
        _rms(o_conv, mg_ref[:, w_att:]).astype(BF16)], axis=-1)
    mix = jnp.dot(cat, wout_ref[...], preferred_element_type=F32)
    x = x_ref[...] + mod_ref[2:3, :] * mix
    xo_ref[...] = x
    hm = _rms(x, n2_ref[...]) * (1.0 + mod_ref[4:5, :]) + mod_ref[3:4, :]
    h_ref[...] = hm
    logits = lax.dot_general(rw_ref[...], hm, (((1,), (1,)), ((), ())),
                             preferred_element_type=F32, precision=HIGHEST)
    e = jnp.exp(logits - jnp.max(logits, axis=0, keepdims=True))
    aff_ref[...] = e / jnp.sum(e, axis=0, keepdims=True)


def _mix(o, u, xs, mod, lw, tile_off):
    B, S_o, _ = o.shape
    D = xs.shape[-1]
    nt_all = u.shape[1] // TM
    nt = S_o // TM
    const2 = lambda b, j: (0, 0)
    weights = [lw["conv_w"], lw["conv_dw_b"], lw["conv_ln_g"], lw["conv_ln_b"], lw["mix_out_g"], lw["w_out"],
               lw["norm2_g"], lw["router_wt"]]
    uspec = lambda d: pl.BlockSpec(
        (None, TM, CONV_CH), lambda b, j: (b, jnp.clip(j + tile_off + d, 0, nt_all - 1), 0))
    return pl.pallas_call(
        functools.partial(_mix_kernel, tile_off=tile_off, last_tile=nt_all - 1),
        grid=(B, nt),
        in_specs=[pl.BlockSpec((None, TM, o.shape[-1]), lambda b, j: (b, j, 0)),
                  uspec(-1), uspec(0), uspec(1),
                  pl.BlockSpec((None, TM, D), lambda b, j: (b, j + tile_off, 0)),
                  pl.BlockSpec((None, 6, D), lambda b, j: (_mod_row(b, j + tile_off), 0, 0))]
                 + [pl.BlockSpec(w.shape, const2) for w in weights],
        out_specs=[pl.BlockSpec((None, TM, D), lambda b, j: (b, j, 0)),
                   pl.BlockSpec((None, TM, D), lambda b, j: (b, j, 0)),
                   pl.BlockSpec((None, N_EXPERTS, TM), lambda b, j: (b, 0, j))],
        out_shape=[jax.ShapeDtypeStruct((B, S_o, D), F32),
                   jax.ShapeDtypeStruct((B, S_o, D), F32),
                   jax.ShapeDtypeStruct((B, N_EXPERTS, S_o), F32)],
        scratch_shapes=[pltpu.VMEM((TM + 2 * HALO, CONV_CH), F32)],
        compiler_params=_cparams(("parallel", "parallel")),
        name="mix",
    )(o, u, u, u, xs, mod, *weights)


def _moe_kernel(idx_ref, h_hbm, g_ref, w1_ref, w3_ref, w2_ref, y_ref, xbuf0, xbuf1, w1b, w3b, w2b, sem,
                *, nchunk, rows, per_it, r_pad):
    e, f = pl.program_id(0), pl.program_id(1)
    ne, nf = pl.num_programs(0), pl.num_programs(1)
    xbufs = (xbuf0, xbuf1)

    def request(expert, r0, n, into):
        for k in range(n):
            src = idx_ref[expert * r_pad + r0 + k]
            pltpu.make_async_copy(h_hbm.at[pl.ds(src, 1)], xbufs[into].at[pl.ds(r0 + k, 1)], sem.at[into]).start()

    def wait_rows(which):
        pltpu.make_async_copy(h_hbm.at[pl.ds(0, r_pad)], xbufs[which], sem.at[which]).wait()

    @pl.when((e == 0) & (f == 0))
    def _():
        def first(i, carry):
            request(0, pl.multiple_of(i * per_it, 8), per_it, 0)
            return carry
        lax.fori_loop(0, r_pad // per_it, first, 0)

    nxt = jnp.minimum(e + 1, ne - 1)

    def run(cur):
        @pl.when(f == 0)
        def _():
            wait_rows(cur)
            y_ref[...] = jnp.zeros_like(y_ref)

        w1b[...] = w1_ref[...].astype(BF16)
        w3b[...] = w3_ref[...].astype(BF16)
        w2b[...] = w2_ref[...].astype(BF16)
        for c in range(nchunk):
            request(nxt, pl.multiple_of((f * nchunk + c) * per_it, 8), per_it, 1 - cur)
            xb = xbufs[cur][c * rows:(c + 1) * rows, :].astype(BF16)
            h1 = jnp.dot(xb, w1b[...], preferred_element_type=F32)
            h3 = jnp.dot(xb, w3b[...], preferred_element_type=F32)
            hid = (h1 * jax.nn.sigmoid(h1) * h3).astype(BF16)
            y_ref[c * rows:(c + 1) * rows, :] += jnp.dot(hid, w2b[...], preferred_element_type=F32)

        @pl.when((e == ne - 1) & (f == nf - 1))
        def _():
            wait_rows(1 - cur)

    for parity in (0, 1):
        pl.when(e % 2 == parity)(functools.partial(run, parity))

    @pl.when(f == nf - 1)
    def _():
        y_ref[...] = y_ref[...] * g_ref[...]


def _moe_tiles(R, FF):
    rows = next(r for r in range(1056, 15, -16) if R % r == 0)
    tf = next(t for t in range(512, LANES - 1, -LANES) if FF % t == 0)
    steps = (R // rows) * (FF // tf)
    per_it = 8 * (-(-R // (8 * steps)))
    return rows, tf, per_it, per_it * steps


def _experts(h_flat, gidx, gate, w1, w3, w2, l):
    E, R = gidx.shape
    D = h_flat.shape[-1]
    FF = w1.shape[-1]
    rows, tf, per_it, r_pad = _moe_tiles(R, FF)
    idx = jnp.pad(gidx, ((0, 0), (0, r_pad - R))).reshape(E * r_pad)
    grid_spec = pltpu.PrefetchScalarGridSpec(
        num_scalar_prefetch=1,
        grid=(E, FF // tf),
        in_specs=[pl.BlockSpec(memory_space=pl.ANY),
                  pl.BlockSpec((None, R, 1), lambda e, f, idx: (e, 0, 0)),
                  pl.BlockSpec((None, None, D, tf), lambda e, f, idx: (l, e, 0, f)),
                  pl.BlockSpec((None, None, D, tf), lambda e, f, idx: (l, e, 0, f)),
                  pl.BlockSpec((None, None, tf, D), lambda e, f, idx: (l, e, f, 0))],
        out_specs=pl.BlockSpec((None, R, D), lambda e, f, idx: (e, 0, 0)),
        scratch_shapes=[pltpu.VMEM((r_pad, D), F32), pltpu.VMEM((r_pad, D), F32),
                        pltpu.VMEM((D, tf), BF16), pltpu.VMEM((D, tf), BF16), pltpu.VMEM((tf, D), BF16),
                        pltpu.SemaphoreType.DMA((2,))])
    return pl.pallas_call(
        functools.partial(_moe_kernel, nchunk=R // rows, rows=rows, per_it=per_it, r_pad=r_pad),
        grid_spec=grid_spec,
        out_shape=jax.ShapeDtypeStruct((E, R, D), F32),
        compiler_params=_cparams(("arbitrary", "arbitrary")),
        name="experts",
    )(idx, h_flat, gate, w1, w3, w2)


def _final_kernel(x_ref, moe_ref, mod_ref, g_ref, o_ref):
    x = x_ref[...] + mod_ref[5:6, :] * moe_ref[...]
    o_ref[...] = _rms(x, g_ref[...])


def _final(x, moe, mod, g):
    B, N, D = x.shape
    tok = lambda b, j: (b, j, 0)
    return pl.pallas_call(
        _final_kernel,
        grid=(B, N // TM),
        in_specs=[pl.BlockSpec((None, TM, D), tok), pl.BlockSpec((None, TM, D), tok),
                  pl.BlockSpec((None, 6, D), lambda b, j: (b, 0, 0)),
                  pl.BlockSpec((1, D), lambda b, j: (0, 0))],
        out_specs=pl.BlockSpec((None, TM, D), tok),
        out_shape=jax.ShapeDtypeStruct((B, N, D), F32),
        compiler_params=_cparams(("parallel", "parallel")),
        name="final_norm",
    )(x, moe, mod, g)


def _pad_cols(w, lo, width=LANES):
    return jnp.pad(w, ((0, 0), (lo, width - lo - w.shape[1])))


def _prep_layer(p):
    w_in = p["w_in"]
    o = 0
    parts = {}
    for name, size in (("cq", MLA_Q_RANK), ("ckv", MLA_KV_RANK), ("kr", MLA_ROPE),
                       ("zq", GQA_HEADS * GQA_HEAD_DIM), ("zk", GQA_KV_HEADS * GQA_HEAD_DIM),
                       ("zv", GQA_KV_HEADS * GQA_HEAD_DIM), ("zc", 2 * CONV_CH)):
        parts[name] = w_in[:, o:o + size]
        o += size
    hd = GQA_HEAD_DIM
    group = GQA_HEADS // GQA_KV_HEADS
    cols = [parts["cq"], parts["ckv"]]
    cols += [_pad_cols(parts["zq"][:, h * hd:(h + 1) * hd], 0) for h in range(GQA_HEADS)]
    cols += [_pad_cols(parts["zk"][:, h * hd:(h + 1) * hd], 0) for h in range(GQA_KV_HEADS)]
    cols += [_pad_cols(parts["zv"][:, (h // group) * hd:(h // group + 1) * hd], (h % 2) * hd)
             for h in range(GQA_HEADS)]
    cols += [parts["zc"], _pad_cols(parts["kr"], MLA_NOPE)]
    w_in_p = jnp.concatenate(cols, axis=1).astype(BF16)

    dq = MLA_NOPE + MLA_ROPE
    w_uq = jnp.concatenate([_pad_cols(p["mla_w_uq"][:, h * dq:(h + 1) * dq], 0) for h in range(MLA_HEADS)],
                           axis=1).astype(BF16)
    dkv = MLA_NOPE + MLA_V
    wk = [_pad_cols(p["mla_w_ukv"][:, h * dkv:h * dkv + MLA_NOPE], 0) for h in range(MLA_HEADS)]
    wv = [_pad_cols(p["mla_w_ukv"][:, h * dkv + MLA_NOPE:(h + 1) * dkv], (h % 2) * MLA_V) for h in range(MLA_HEADS)]
    w_ukv = jnp.concatenate(wk + wv, axis=1).astype(BF16)

    row = lambda a: a.reshape(1, -1)
    return {
        "norm1_g": row(p["norm1_g"]), "w_in": w_in_p,
        "mla_q_norm_g": row(p["mla_q_norm_g"]), "w_uq": w_uq,
        "mla_kv_norm_g": row(p["mla_kv_norm_g"]), "w_ukv": w_ukv,
        "gqa_q_norm_g": _pad_cols(row(p["gqa_q_norm_g"]), 0), "gqa_k_norm_g": _pad_cols(row(p["gqa_k_norm_g"]), 0),
        "conv_w": jnp.pad(p["conv_dw_w"], ((0, 32 - CONV_WIDTH), (0, 0))),
        "conv_dw_b": row(p["conv_dw_b"]), "conv_ln_g": row(p["conv_ln_g"]), "conv_ln_b": row(p["conv_ln_b"]),
        "mix_out_g": row(p["mix_out_g"]), "w_out": p["w_out"].astype(BF16),
        "norm2_g": row(p["norm2_g"]), "router_wt": p["router_w"].T,
    }


def _rope_tables(n, n_ctx):
    rows = n // GRID_W
    row = jnp.repeat(jnp.arange(rows), GRID_W).astype(F32)
    col = jnp.tile(jnp.arange(GRID_W), rows).astype(F32)
    tabs = []
    for d_rot, lo in ((MLA_ROPE, MLA_NOPE), (GQA_HEAD_DIM, 0)):
        d_axis = d_rot // 2
        inv = ROPE_BASE ** (-jnp.arange(0, d_axis, 2, dtype=F32) / d_axis)
        ang = jnp.concatenate([row[:, None] * inv, col[:, None] * inv], axis=-1)
        cos, sin = jnp.cos(ang), jnp.sin(ang)
        cos_b = jnp.pad(jnp.concatenate([cos, cos], axis=1) - 1.0, ((n_ctx, 0), (lo, LANES - lo - d_rot))) + 1.0
        sin_b = jnp.pad(jnp.concatenate([-sin, sin], axis=1), ((n_ctx, 0), (lo, LANES - lo - d_rot)))
        tabs += [cos_b, sin_b]
    return tabs


def _route(aff, stride, cap, row0):
    gate, idx = lax.top_k(aff, cap)
    B = aff.shape[0]
    gidx = idx + (row0 + jnp.arange(B, dtype=idx.dtype) * stride)[:, None, None]
    E = aff.shape[1]
    return gidx.transpose(1, 0, 2).reshape(E, B * cap), gate.transpose(1, 0, 2).reshape(E, B * cap)


def _moe_layer(aff, h, n_ctx, with_ctx, w1, w3, w2, l):
    B, S_o, D = h.shape
    lat0 = n_ctx if with_ctx else 0
    n = S_o - lat0
    sets = [_route(aff[:, :, lat0:], S_o, CAPACITY_FACTOR * n // N_EXPERTS, lat0)]
    if with_ctx:
        sets.append(_route(aff[:, :, :n_ctx], S_o, CAPACITY_FACTOR * n_ctx // N_EXPERTS, 0))
    gidx = jnp.concatenate([s[0] for s in sets], axis=1)
    gate = jnp.concatenate([s[1] for s in sets], axis=1)
    y = _experts(h.reshape(B * S_o, D), gidx, gate[..., None], w1, w3, w2, l)
    out = jnp.zeros((B * S_o, D), F32).at[gidx.reshape(-1)].add(y.reshape(-1, D))
    return out.reshape(B, S_o, D)


def kernel(x, c, ctx, c_ctx, mod_w, mod_b, norm1_g, w_in, mla_q_norm_g, mla_w_uq, mla_kv_norm_g, mla_w_ukv,
           gqa_q_norm_g, gqa_k_norm_g, conv_dw_w, conv_dw_b, conv_ln_g, conv_ln_b, mix_out_g, w_out, norm2_g,
           router_w, exp_w1, exp_w3, exp_w2, final_g):
    B, N, D = x.shape
    n_ctx = ctx.shape[1]
    depth = mod_w.shape[0]
    assert n_ctx == TM == TQ and N % TM == 0
    stacked = {"norm1_g": norm1_g, "w_in": w_in, "mla_q_norm_g": mla_q_norm_g, "mla_w_uq": mla_w_uq,
               "mla_kv_norm_g": mla_kv_norm_g, "mla_w_ukv": mla_w_ukv, "gqa_q_norm_g": gqa_q_norm_g,
               "gqa_k_norm_g": gqa_k_norm_g, "conv_dw_w": conv_dw_w, "conv_dw_b": conv_dw_b,
               "conv_ln_g": conv_ln_g, "conv_ln_b": conv_ln_b, "mix_out_g": mix_out_g, "w_out": w_out,
               "norm2_g": norm2_g, "router_w": router_w}

    assert B <= CTX_ROW
    cvec = jnp.zeros((MOD_ROWS, D), F32).at[:B].set(c).at[CTX_ROW].set(c_ctx)
    mods = _modulation(cvec, mod_w, mod_b).reshape(depth, MOD_ROWS, 6, D)
    tabs = _rope_tables(N, n_ctx)

    xs = jnp.concatenate([ctx, x], axis=1)
    moe, modp = None, None
    for l in range(depth):
        lw = _prep_layer({k: v[l] for k, v in stacked.items()})
        last = l == depth - 1
        tile_off = 1 if last else 0
        xs, (q, k, v, u) = _project(xs, moe, modp, mods[l], lw, tabs)
        o = _attention(q, k, v, n_ctx, tile_off)
        xs, h, aff = _mix(o, u, xs, mods[l], lw, tile_off)
        moe = _moe_layer(aff, h, n_ctx, not last, exp_w1, exp_w3, exp_w2, l)
        modp = mods[l]
    return _final(xs, moe, modp, final_g.reshape(1, D))
```

```python
import functools
import math

import jax
import jax.numpy as jnp
from jax import lax
from jax.experimental import pallas as pl
from jax.experimental.pallas import tpu as pltpu

F32 = jnp.float32
BF16 = jnp.bfloat16
HIGHEST = lax.Precision.HIGHEST

GRID_W = 64
ROPE_BASE = 10000.0
EPS = 1e-6

MLA_HEADS = 8
MLA_Q_RANK = 256
MLA_KV_RANK = 128
MLA_NOPE = 64
MLA_ROPE = 32
MLA_V = 64
GQA_HEADS = 4
GQA_KV_HEADS = 2
GQA_HEAD_DIM = 64
CONV_CH = 256
CONV_WIDTH = 31
N_EXPERTS = 16
CAPACITY_FACTOR = 2

MLA_SCALE = 1.0 / math.sqrt(MLA_NOPE + MLA_ROPE)
GQA_SCALE = 1.0 / math.sqrt(GQA_HEAD_DIM)

LANES = 128
N_QHEADS = MLA_HEADS + GQA_HEADS
N_KHEADS = MLA_HEADS + GQA_KV_HEADS
N_PAIRS = N_QHEADS // 2
TM = 256
TQ = 256
TK_MAX = 2816
ONES_LANE = (MLA_V, 0)
LOG2E = math.log2(math.e)
HALO = 16
VMEM_LIMIT = 56 * 1024 * 1024
MOD_ROWS = 8
CTX_ROW = MOD_ROWS - 1

C_CQ = 0
C_CKV = C_CQ + MLA_Q_RANK
C_ZQ = C_CKV + MLA_KV_RANK
C_ZK = C_ZQ + GQA_HEADS * LANES
C_ZV = C_ZK + GQA_KV_HEADS * LANES
C_ZA = C_ZV + GQA_HEADS * LANES
C_ZB = C_ZA + CONV_CH
C_KR = C_ZB + CONV_CH
IN_PAD = C_KR + LANES


def _cparams(sem):
    return pltpu.CompilerParams(dimension_semantics=sem, vmem_limit_bytes=VMEM_LIMIT)


def _rms(x, g):
    return x * lax.rsqrt(jnp.mean(x * x, axis=-1, keepdims=True) + EPS) * g


def _head_rms(x, g, width):
    return x * lax.rsqrt(jnp.sum(x * x, axis=-1, keepdims=True) * (1.0 / width) + EPS) * g


def _rope(x, cos, sin, lo, half):
    lane = lax.broadcasted_iota(jnp.int32, x.shape, 1)
    first = (lane >= lo) & (lane < lo + half)
    rot = jnp.where(first, pltpu.roll(x, LANES - half, 1), pltpu.roll(x, half, 1))
    return x * cos + rot * sin


def _ones_col(h):
    lane = lax.broadcasted_iota(jnp.int32, (1, LANES), 1)
    return (lane == ONES_LANE[h % 2]).astype(F32)


def _mod_kernel(c_ref, w_ref, b_ref, o_ref):
    c = c_ref[...]
    a = c * jax.nn.sigmoid(c)
    o_ref[...] = jnp.dot(a, w_ref[...], preferred_element_type=F32, precision=HIGHEST) + b_ref[...]


def _modulation(cvec, mod_w, mod_b):
    L, D, W = mod_w.shape
    tn = 1536
    return pl.pallas_call(
        _mod_kernel,
        grid=(L, W // tn),
        in_specs=[pl.BlockSpec((MOD_ROWS, D), lambda l, n: (0, 0)),
                  pl.BlockSpec((None, D, tn), lambda l, n: (l, 0, n)),
                  pl.BlockSpec((None, 1, tn), lambda l, n: (l, 0, n))],
        out_specs=pl.BlockSpec((None, MOD_ROWS, tn), lambda l, n: (l, 0, n)),
        out_shape=jax.ShapeDtypeStruct((L, MOD_ROWS, W), F32),
        compiler_params=_cparams(("arbitrary", "arbitrary")),
        name="modulation",
    )(cvec, mod_w, mod_b.reshape(L, 1, W))


def _proj_kernel(*refs, has_moe):
    if has_moe:
        x_ref, moe_ref, modp_ref, refs = refs[0], refs[1], refs[2], refs[3:]
    else:
        x_ref, refs = refs[0], refs[1:]
    (mod_ref, n1_ref, win_ref, gq_ref, wuq_ref, gkv_ref, wukv_ref, ggq_ref, ggk_ref,
     cm_ref, sm_ref, cg_ref, sg_ref) = refs[:13]
    outs = refs[13:]
    if has_moe:
        xo_ref, outs = outs[0], outs[1:]
    q_ref, k_ref, v_ref, u_ref = outs

    x = x_ref[...]
    if has_moe:
        x = x + modp_ref[5:6, :] * moe_ref[...]
        xo_ref[...] = x
    hm = _rms(x, n1_ref[...]) * (1.0 + mod_ref[1:2, :]) + mod_ref[0:1, :]
    z = jnp.dot(hm.astype(BF16), win_ref[...], preferred_element_type=F32)

    cos_m, sin_m, cos_g, sin_g = cm_ref[...], sm_ref[...], cg_ref[...], sg_ref[...]

    cqn = _rms(z[:, C_CQ:C_CQ + MLA_Q_RANK], gq_ref[...])
    q = jnp.dot(cqn.astype(BF16), wuq_ref[...], preferred_element_type=F32)
    for h in range(MLA_HEADS):
        q_ref[h] = _rope(q[:, h * LANES:(h + 1) * LANES], cos_m, sin_m, MLA_NOPE, MLA_ROPE // 2).astype(BF16)

    ckvn = _rms(z[:, C_CKV:C_CKV + MLA_KV_RANK], gkv_ref[...])
    kv = jnp.dot(ckvn.astype(BF16), wukv_ref[...], preferred_element_type=F32)
    kr = _rope(z[:, C_KR:C_KR + LANES], cos_m, sin_m, MLA_NOPE, MLA_ROPE // 2)
    for h in range(MLA_HEADS):
        k_ref[h] = (kv[:, h * LANES:(h + 1) * LANES] + kr).astype(BF16)
        v_ref[h] = (kv[:, (MLA_HEADS + h) * LANES:(MLA_HEADS + h + 1) * LANES] + _ones_col(h)).astype(BF16)

    for h in range(GQA_HEADS):
        zq = z[:, C_ZQ + h * LANES:C_ZQ + (h + 1) * LANES]
        q_ref[MLA_HEADS + h] = _rope(_head_rms(zq, ggq_ref[...], GQA_HEAD_DIM), cos_g, sin_g,
                                     0, GQA_HEAD_DIM // 2).astype(BF16)
        v_ref[MLA_HEADS + h] = (z[:, C_ZV + h * LANES:C_ZV + (h + 1) * LANES] + _ones_col(h)).astype(BF16)
    for h in range(GQA_KV_HEADS):
        zk = z[:, C_ZK + h * LANES:C_ZK + (h + 1) * LANES]
        k_ref[MLA_HEADS + h] = _rope(_head_rms(zk, ggk_ref[...], GQA_HEAD_DIM), cos_g, sin_g,
                                     0, GQA_HEAD_DIM // 2).astype(BF16)

    u_ref[...] = z[:, C_ZA:C_ZA + CONV_CH] * jax.nn.sigmoid(z[:, C_ZB:C_ZB + CONV_CH])


def _mod_row(b, j):
    return jnp.where(j == 0, CTX_ROW, b)


def _project(xs, moe, modp, mod, lw, tabs):
    B, S, D = xs.shape
    nt = S // TM
    has_moe = moe is not None
    tok = lambda b, j: (b, j, 0)
    const2 = lambda b, j: (0, 0)
    modspec = pl.BlockSpec((None, 6, D), lambda b, j: (_mod_row(b, j), 0, 0))
    in_specs = [pl.BlockSpec((None, TM, D), tok)]
    args = [xs]
    if has_moe:
        in_specs += [pl.BlockSpec((None, TM, D), tok), modspec]
        args += [moe, modp]
    weights = [lw["norm1_g"], lw["w_in"], lw["mla_q_norm_g"], lw["w_uq"], lw["mla_kv_norm_g"], lw["w_ukv"],
               lw["gqa_q_norm_g"], lw["gqa_k_norm_g"]]
    in_specs += [modspec] + [pl.BlockSpec(w.shape, const2) for w in weights]
    in_specs += [pl.BlockSpec((TM, LANES), lambda b, j: (j, 0))] * 4
    args += [mod] + weights + list(tabs)
    head_spec = lambda n: pl.BlockSpec((None, n, TM, LANES), lambda b, j: (b, 0, j, 0))
    out_specs = [head_spec(N_QHEADS), head_spec(N_KHEADS), head_spec(N_QHEADS),
                 pl.BlockSpec((None, TM, CONV_CH), tok)]
    out_shape = [jax.ShapeDtypeStruct((B, N_QHEADS, S, LANES), BF16),
                 jax.ShapeDtypeStruct((B, N_KHEADS, S, LANES), BF16),
                 jax.ShapeDtypeStruct((B, N_QHEADS, S, LANES), BF16),
                 jax.ShapeDtypeStruct((B, S, CONV_CH), F32)]
    if has_moe:
        out_specs = [pl.BlockSpec((None, TM, D), tok)] + out_specs
        out_shape = [jax.ShapeDtypeStruct((B, S, D), F32)] + out_shape
    outs = pl.pallas_call(
        functools.partial(_proj_kernel, has_moe=has_moe),
        grid=(B, nt), in_specs=in_specs, out_specs=out_specs, out_shape=out_shape,
        compiler_params=_cparams(("parallel", "parallel")),
        name="project",
    )(*args)
    if has_moe:
        return outs[0], outs[1:]
    return xs, outs


def _scores(q, k):
    return lax.dot_general(q, k, (((1,), (1,)), ((), ())), preferred_element_type=F32)


def _softmax_update(s, v, m, acc, c):
    m_new = jnp.maximum(m, jnp.max(s, axis=-1, keepdims=True))
    alpha = jnp.exp2((m - m_new) * c)
    p = jnp.exp2((s - m_new) * c).astype(BF16)
    acc = alpha * acc + jnp.dot(p, v, preferred_element_type=F32)
    return m_new, acc


def _attn_kernel(q_ref, k0_ref, k1_ref, v_ref, o_ref, s_ref, *, tile_off, n_ctx, n_all, tk):
    hp = pl.program_id(1)
    j = pl.program_id(2) + tile_off
    c = jnp.where(hp < MLA_HEADS // 2, MLA_SCALE * LOG2E, GQA_SCALE * LOG2E).astype(F32)
    q0, q1 = q_ref[0], q_ref[1]
    init = (jnp.full((TQ, 1), -jnp.inf, F32), jnp.zeros((TQ, LANES), F32))

    def finish(acc0, acc1):
        lane = lax.broadcasted_iota(jnp.int32, (TQ, LANES), 1)
        l0 = acc0[:, ONES_LANE[0]:ONES_LANE[0] + 1]
        l1 = acc1[:, ONES_LANE[1]:ONES_LANE[1] + 1]
        o_ref[...] = jnp.where(lane < LANES // 2, acc0 / l0, acc1 / l1)

    @pl.when(j == 0)
    def _():
        _, acc0 = _softmax_update(_scores(q0, k0_ref[0:n_ctx, :]), v_ref[0, 0:n_ctx, :], *init, c)
        _, acc1 = _softmax_update(_scores(q1, k1_ref[0:n_ctx, :]), v_ref[1, 0:n_ctx, :], *init, c)
        finish(acc0, acc1)

    @pl.when(j > 0)
    def _():
        n = n_all // tk

        def put(slot, r):
            s_ref[slot, 0] = _scores(q0, k0_ref[pl.ds(r, tk), :])
            s_ref[slot, 1] = _scores(q1, k1_ref[pl.ds(r, tk), :])

        def step(slot, r, carry):
            m0, a0, m1, a1 = carry
            m0, a0 = _softmax_update(s_ref[slot, 0], v_ref[0, pl.ds(r, tk), :], m0, a0, c)
            m1, a1 = _softmax_update(s_ref[slot, 1], v_ref[1, pl.ds(r, tk), :], m1, a1, c)
            return m0, a0, m1, a1

        def body(t, carry):
            r0 = pl.multiple_of(2 * t * tk, tk)
            r1 = pl.multiple_of(r0 + tk, tk)
            r2 = pl.multiple_of(r1 + tk, tk)
            put(1, r1)
            carry = step(0, r0, carry)
            put(0, r2)
            return step(1, r1, carry)

        put(0, 0)
        paired = (n - 1) // 2
        carry = lax.fori_loop(0, paired, body, init + init)
        for t in range(2 * paired, n):
            if t + 1 < n:
                put((t + 1) % 2, (t + 1) * tk)
            carry = step(t % 2, t * tk, carry)
        finish(carry[1], carry[3])


def _kv_head(h):
    group = GQA_HEADS // GQA_KV_HEADS
    return jnp.where(h < MLA_HEADS, h, MLA_HEADS + (h - MLA_HEADS) // group)


def _attention(q, k, v, n_ctx, tile_off):
    B, _, S, _ = q.shape
    nq = S // TQ - tile_off
    tk = next(t for t in range(TK_MAX, 0, -LANES) if S % t == 0)
    kspec = lambda i: pl.BlockSpec((None, None, S, LANES), lambda b, hp, j: (b, _kv_head(2 * hp + i), 0, 0))
    return pl.pallas_call(
        functools.partial(_attn_kernel, tile_off=tile_off, n_ctx=n_ctx, n_all=S, tk=tk),
        grid=(B, N_PAIRS, nq),
        in_specs=[pl.BlockSpec((None, 2, TQ, LANES), lambda b, hp, j: (b, hp, j + tile_off, 0)),
                  kspec(0), kspec(1),
                  pl.BlockSpec((None, 2, S, LANES), lambda b, hp, j: (b, hp, 0, 0))],
        out_specs=pl.BlockSpec((None, TQ, LANES), lambda b, hp, j: (b, j, hp)),
        out_shape=jax.ShapeDtypeStruct((B, nq * TQ, N_PAIRS * LANES), F32),
        scratch_shapes=[pltpu.VMEM((2, 2, TQ, tk), F32)],
        compiler_params=_cparams(("parallel", "parallel", "arbitrary")),
        name="attention",
    )(q, k, k, v)


def _mix_kernel(o_ref, up_ref, uc_ref, un_ref, x_ref, mod_ref, cw_ref, cb_ref, lg_ref, lb_ref,
                mg_ref, wout_ref, n2_ref, rw_ref,
                xo_ref, h_ref, aff_ref, ext_ref, *, tile_off, last_tile):
    j = pl.program_id(1) + tile_off
    has_prev = j >= 2
    has_next = (j >= 1) & (j < last_tile)
    ext_ref[0:HALO, :] = jnp.where(has_prev, up_ref[TM - HALO:TM, :], 0.0)
    ext_ref[HALO:HALO + TM, :] = uc_ref[...]
    ext_ref[HALO + TM:HALO + TM + HALO, :] = jnp.where(has_next, un_ref[0:HALO, :], 0.0)
    y = jnp.zeros((TM, CONV_CH), F32) + cb_ref[...]
    for t in range(CONV_WIDTH):
        r = HALO - CONV_WIDTH // 2 + t
        y = y + ext_ref[r:r + TM, :] * cw_ref[t:t + 1, :]
    mu = jnp.mean(y, axis=-1, keepdims=True)
    yc = y - mu
    yn = yc * lax.rsqrt(jnp.mean(yc * yc, axis=-1, keepdims=True) + EPS) * lg_ref[...] + lb_ref[...]
    o_conv = yn * jax.nn.sigmoid(yn)

    o = o_ref[...]
    w_mla = MLA_HEADS * MLA_V
    w_att = w_mla + GQA_HEADS * GQA_HEAD_DIM
    cat = jnp.concatenate([
        _rms(o[:, :w_mla], mg_ref[:, :w_mla]).astype(BF16),
        _rms(o[:, w_mla:w_att], mg_ref[:, w_mla:w_att]).astype(BF16),
        _rms(o_conv, mg_ref[:, w_att:]).astype(BF16)], axis=-1)
    mix = jnp.dot(cat, wout_ref[...], preferred_element_type=F32)
    x = x_ref[...] + mod_ref[2:3, :] * mix
    xo_ref[...] = x
    hm = _rms(x, n2_ref[...]) * (1.0 + mod_ref[4:5, :]) + mod_ref[3:4, :]
    h_ref[...] = hm
    logits = lax.dot_general(rw_ref[...], hm, (((1,), (1,)), ((), ())),
                             preferred_element_type=F32, precision=HIGHEST)
    e = jnp.exp(logits - jnp.max(logits, axis=0, keepdims=True))
    aff_ref[...] = e / jnp.sum(e, axis=0, keepdims=True)


def _mix(o, u, xs, mod, lw, tile_off):
    B, S_o, _ = o.shape
    D = xs.shape[-1]
    nt_all = u.shape[1] // TM
    nt = S_o // TM
    const2 = lambda b, j: (0, 0)
    weights = [lw["conv_w"], lw["conv_dw_b"], lw["conv_ln_g"], lw["conv_ln_b"], lw["mix_out_g"], lw["w_out"],
               lw["norm2_g"], lw["router_wt"]]
    uspec = lambda d: pl.BlockSpec(
        (None, TM, CONV_CH), lambda b, j: (b, jnp.clip(j + tile_off + d, 0, nt_all - 1), 0))
    return pl.pallas_call(
        functools.partial(_mix_kernel, tile_off=tile_off, last_tile=nt_all - 1),
        grid=(B, nt),
        in_specs=[pl.BlockSpec((None, TM, o.shape[-1]), lambda b, j: (b, j, 0)),
                  uspec(-1), uspec(0), uspec(1),
                  pl.BlockSpec((None, TM, D), lambda b, j: (b, j + tile_off, 0)),
                  pl.BlockSpec((None, 6, D), lambda b, j: (_mod_row(b, j + tile_off), 0, 0))]
                 + [pl.BlockSpec(w.shape, const2) for w in weights],
        out_specs=[pl.BlockSpec((None, TM, D), lambda b, j: (b, j, 0)),
                   pl.BlockSpec((None, TM, D), lambda b, j: (b, j, 0)),
                   pl.BlockSpec((None, N_EXPERTS, TM), lambda b, j: (b, 0, j))],
        out_shape=[jax.ShapeDtypeStruct((B, S_o, D), F32),
                   jax.ShapeDtypeStruct((B, S_o, D), F32),
                   jax.ShapeDtypeStruct((B, N_EXPERTS, S_o), F32)],
        scratch_shapes=[pltpu.VMEM((TM + 2 * HALO, CONV_CH), F32)],
        compiler_params=_cparams(("parallel", "parallel")),
        name="mix",
    )(o, u, u, u, xs, mod, *weights)


def _moe_kernel(idx_ref, h_hbm, out0_hbm, g_ref, w1_ref, w3_ref, w2_ref, out_hbm,
                xbuf0, xbuf1, acc0, acc1, obuf, w1b, w3b, w2b, sx, so,
                *, rows, n_tiles, px, x_pad, n_put, r_put, n_get, r_get):
    del out0_hbm
    v, f = pl.program_id(0), pl.program_id(1)
    nv, nf = pl.num_programs(0), n_tiles
    xbufs, accs = (xbuf0, xbuf1), (acc0, acc1)
    prv = jnp.maximum(v - 1, 0)
    nxt = jnp.minimum(v + 1, nv - 1)

    def request(owner, r0, n, into):
        for k in range(n):
            src = idx_ref[owner * x_pad + r0 + k]
            pltpu.make_async_copy(h_hbm.at[pl.ds(src, 1)], xbufs[into].at[pl.ds(r0 + k, 1)], sx.at[into]).start()

    def put(owner, r0, n, frm):
        for k in range(n):
            dst = idx_ref[owner * x_pad + r0 + k]
            pltpu.make_async_copy(accs[frm].at[pl.ds(r0 + k, 1)], out_hbm.at[pl.ds(dst, 1)], so.at[0]).start()

    def get(owner, r0, n):
        for k in range(n):
            src = idx_ref[owner * x_pad + r0 + k]
            pltpu.make_async_copy(out_hbm.at[pl.ds(src, 1)], obuf.at[pl.ds(r0 + k, 1)], so.at[1]).start()

    def wait_requests(which):
        pltpu.make_async_copy(h_hbm.at[pl.ds(0, x_pad)], xbufs[which], sx.at[which]).wait()

    def wait_puts():
        pltpu.make_async_copy(acc0, out_hbm.at[pl.ds(0, rows)], so.at[0]).wait()

    def wait_gets():
        pltpu.make_async_copy(out_hbm.at[pl.ds(0, rows)], obuf, so.at[1]).wait()

    @pl.when((v == 0) & (f == 0))
    def _():
        def first(i, carry):
            request(0, pl.multiple_of(i * px, 8), px, 0)
            return carry
        lax.fori_loop(0, x_pad // px, first, 0)
        acc1[...] = jnp.zeros_like(acc1)

    @pl.when(f == n_put)
    def _():
        wait_puts()

    def run(cur, phase):
        @pl.when(f == 0)
        def _():
            wait_requests(cur)
            accs[cur][...] = jnp.zeros_like(accs[cur])

        w1b[...] = w1_ref[...].astype(BF16)
        w3b[...] = w3_ref[...].astype(BF16)
        w2b[...] = w2_ref[...].astype(BF16)
        request(nxt, pl.multiple_of(f * px, 8), px, 1 - cur)
        if phase == "put":
            put(prv, pl.multiple_of(f * r_put, 8), r_put, 1 - cur)
        elif phase == "get":
            get(v, pl.multiple_of((f - n_put) * r_get, 8), r_get)
        xb = xbufs[cur][0:rows, :].astype(BF16)
        h1 = jnp.dot(xb, w1b[...], preferred_element_type=F32)
        h3 = jnp.dot(xb, w3b[...], preferred_element_type=F32)
        hid = (h1 * jax.nn.sigmoid(h1) * h3).astype(BF16)
        accs[cur][...] += g_ref[...] * jnp.dot(hid, w2b[...], preferred_element_type=F32)

        @pl.when(f == nf - 1)
        def _():
            wait_gets()
            accs[cur][...] += obuf[...]

        @pl.when((v == nv - 1) & (f == nf - 1))
        def _():
            wait_requests(1 - cur)

            def last(i, carry):
                put(v, pl.multiple_of(i * r_put, 8), r_put, cur)
                return carry
            lax.fori_loop(0, n_put, last, 0)
            wait_puts()

    phases = [("put", f < n_put), ("get", (f >= n_put) & (f < n_put + n_get))]
    if n_put + n_get < n_tiles:
        phases.append(("idle", f >= n_put + n_get))
    for parity in (0, 1):
        for phase, cond in phases:
            pl.when((v % 2 == parity) & cond)(functools.partial(run, parity, phase))


def _split(groups, most):
    return next(n for n in range(most, 0, -1) if groups % n == 0)


def _moe_tiles(R, FF):
    tf = next(t for t in range(512, LANES - 1, -LANES) if FF % t == 0)
    nf = FF // tf
    rows = R // 2
    assert R % 32 == 0 and nf >= 2
    px = 8 * (-(-rows // (8 * nf)))
    n_put = _split(rows // 8, min(4, nf - 1))
    n_get = _split(rows // 8, nf - n_put)
    return dict(rows=rows, n_tiles=nf, px=px, x_pad=px * nf,
                n_put=n_put, r_put=rows // n_put, n_get=n_get, r_get=rows // n_get), tf


def _experts(h_flat, gidx, gate, w1, w3, w2, l):
    E, R = gidx.shape
    T, D = h_flat.shape
    FF = w1.shape[-1]
    t, tf = _moe_tiles(R, FF)
    rows, x_pad = t["rows"], t["x_pad"]
    nv = 2 * E
    idx = jnp.pad(gidx.reshape(nv, rows), ((0, 0), (0, x_pad - rows))).reshape(nv * x_pad)
    grid_spec = pltpu.PrefetchScalarGridSpec(
        num_scalar_prefetch=1,
        grid=(nv, FF // tf),
        in_specs=[pl.BlockSpec(memory_space=pl.ANY),
                  pl.BlockSpec(memory_space=pl.ANY),
                  pl.BlockSpec((None, rows, 1), lambda v, f, idx: (v, 0, 0)),
                  pl.BlockSpec((None, None, D, tf), lambda v, f, idx: (l, v // 2, 0, f)),
                  pl.BlockSpec((None, None, D, tf), lambda v, f, idx: (l, v // 2, 0, f)),
                  pl.BlockSpec((None, None, tf, D), lambda v, f, idx: (l, v // 2, f, 0))],
        out_specs=pl.BlockSpec(memory_space=pl.ANY),
        scratch_shapes=[pltpu.VMEM((x_pad, D), F32), pltpu.VMEM((x_pad, D), F32),
                        pltpu.VMEM((rows, D), F32), pltpu.VMEM((rows, D), F32), pltpu.VMEM((rows, D), F32),
                        pltpu.VMEM((D, tf), BF16), pltpu.VMEM((D, tf), BF16), pltpu.VMEM((tf, D), BF16),
                        pltpu.SemaphoreType.DMA((2,)), pltpu.SemaphoreType.DMA((2,))])
    return pl.pallas_call(
        functools.partial(_moe_kernel, **t),
        grid_spec=grid_spec,
        out_shape=jax.ShapeDtypeStruct((T, D), F32),
        input_output_aliases={2: 0},
        compiler_params=_cparams(("arbitrary", "arbitrary")),
        name="experts",
    )(idx, h_flat, jnp.zeros((T, D), F32), gate.reshape(nv, rows, 1), w1, w3, w2)


def _final_kernel(x_ref, moe_ref, mod_ref, g_ref, o_ref):
    x = x_ref[...] + mod_ref[5:6, :] * moe_ref[...]
    o_ref[...] = _rms(x, g_ref[...])


def _final(x, moe, mod, g):
    B, N, D = x.shape
    tok = lambda b, j: (b, j, 0)
    return pl.pallas_call(
        _final_kernel,
        grid=(B, N // TM),
        in_specs=[pl.BlockSpec((None, TM, D), tok), pl.BlockSpec((None, TM, D), tok),
                  pl.BlockSpec((None, 6, D), lambda b, j: (b, 0, 0)),
                  pl.BlockSpec((1, D), lambda b, j: (0, 0))],
        out_specs=pl.BlockSpec((None, TM, D), tok),
        out_shape=jax.ShapeDtypeStruct((B, N, D), F32),
        compiler_params=_cparams(("parallel", "parallel")),
        name="final_norm",
    )(x, moe, mod, g)


def _pad_cols(w, lo, width=LANES):
    return jnp.pad(w, ((0, 0), (lo, width - lo - w.shape[1])))


def _prep_layer(p):
    w_in = p["w_in"]
    o = 0
    parts = {}
    for name, size in (("cq", MLA_Q_RANK), ("ckv", MLA_KV_RANK), ("kr", MLA_ROPE),
                       ("zq", GQA_HEADS * GQA_HEAD_DIM), ("zk", GQA_KV_HEADS * GQA_HEAD_DIM),
                       ("zv", GQA_KV_HEADS * GQA_HEAD_DIM), ("zc", 2 * CONV_CH)):
        parts[name] = w_in[:, o:o + size]
        o += size
    hd = GQA_HEAD_DIM
    group = GQA_HEADS // GQA_KV_HEADS
    cols = [parts["cq"], parts["ckv"]]
    cols += [_pad_cols(parts["zq"][:, h * hd:(h + 1) * hd], 0) for h in range(GQA_HEADS)]
    cols += [_pad_cols(parts["zk"][:, h * hd:(h + 1) * hd], 0) for h in range(GQA_KV_HEADS)]
    cols += [_pad_cols(parts["zv"][:, (h // group) * hd:(h // group + 1) * hd], (h % 2) * hd)
             for h in range(GQA_HEADS)]
    cols += [parts["zc"], _pad_cols(parts["kr"], MLA_NOPE)]
    w_in_p = jnp.concatenate(cols, axis=1).astype(BF16)

    dq = MLA_NOPE + MLA_ROPE
    w_uq = jnp.concatenate([_pad_cols(p["mla_w_uq"][:, h * dq:(h + 1) * dq], 0) for h in range(MLA_HEADS)],
                           axis=1).astype(BF16)
    dkv = MLA_NOPE + MLA_V
    wk = [_pad_cols(p["mla_w_ukv"][:, h * dkv:h * dkv + MLA_NOPE], 0) for h in range(MLA_HEADS)]
    wv = [_pad_cols(p["mla_w_ukv"][:, h * dkv + MLA_NOPE:(h + 1) * dkv], (h % 2) * MLA_V) for h in range(MLA_HEADS)]
    w_ukv = jnp.concatenate(wk + wv, axis=1).astype(BF16)

    row = lambda a: a.reshape(1, -1)
    return {
        "norm1_g": row(p["norm1_g"]), "w_in": w_in_p,
        "mla_q_norm_g": row(p["mla_q_norm_g"]), "w_uq": w_uq,
        "mla_kv_norm_g": row(p["mla_kv_norm_g"]), "w_ukv": w_ukv,
        "gqa_q_norm_g": _pad_cols(row(p["gqa_q_norm_g"]), 0), "gqa_k_norm_g": _pad_cols(row(p["gqa_k_norm_g"]), 0),
        "conv_w": jnp.pad(p["conv_dw_w"], ((0, 32 - CONV_WIDTH), (0, 0))),
        "conv_dw_b": row(p["conv_dw_b"]), "conv_ln_g": row(p["conv_ln_g"]), "conv_ln_b": row(p["conv_ln_b"]),
        "mix_out_g": row(p["mix_out_g"]), "w_out": p["w_out"].astype(BF16),
        "norm2_g": row(p["norm2_g"]), "router_wt": p["router_w"].T,
    }


def _rope_tables(n, n_ctx):
    rows = n // GRID_W
    row = jnp.repeat(jnp.arange(rows), GRID_W).astype(F32)
    col = jnp.tile(jnp.arange(GRID_W), rows).astype(F32)
    tabs = []
    for d_rot, lo in ((MLA_ROPE, MLA_NOPE), (GQA_HEAD_DIM, 0)):
        d_axis = d_rot // 2
        inv = ROPE_BASE ** (-jnp.arange(0, d_axis, 2, dtype=F32) / d_axis)
        ang = jnp.concatenate([row[:, None] * inv, col[:, None] * inv], axis=-1)
        cos, sin = jnp.cos(ang), jnp.sin(ang)
        cos_b = jnp.pad(jnp.concatenate([cos, cos], axis=1) - 1.0, ((n_ctx, 0), (lo, LANES - lo - d_rot))) + 1.0
        sin_b = jnp.pad(jnp.concatenate([-sin, sin], axis=1), ((n_ctx, 0), (lo, LANES - lo - d_rot)))
        tabs += [cos_b, sin_b]
    return tabs


def _route(aff, stride, cap, row0):
    gate, idx = lax.top_k(aff, cap)
    B = aff.shape[0]
    gidx = idx + (row0 + jnp.arange(B, dtype=idx.dtype) * stride)[:, None, None]
    E = aff.shape[1]
    return gidx.transpose(1, 0, 2).reshape(E, B * cap), gate.transpose(1, 0, 2).reshape(E, B * cap)


def _moe_layer(aff, h, n_ctx, with_ctx, w1, w3, w2, l):
    B, S_o, D = h.shape
    lat0 = n_ctx if with_ctx else 0
    n = S_o - lat0
    sets = [_route(aff[:, :, lat0:], S_o, CAPACITY_FACTOR * n // N_EXPERTS, lat0)]
    if with_ctx:
        sets.append(_route(aff[:, :, :n_ctx], S_o, CAPACITY_FACTOR * n_ctx // N_EXPERTS, 0))
    gidx = jnp.concatenate([s[0] for s in sets], axis=1)
    gate = jnp.concatenate([s[1] for s in sets], axis=1)
    return _experts(h.reshape(B * S_o, D), gidx, gate, w1, w3, w2, l).reshape(B, S_o, D)


def kernel(x, c, ctx, c_ctx, mod_w, mod_b, norm1_g, w_in, mla_q_norm_g, mla_w_uq, mla_kv_norm_g, mla_w_ukv,
           gqa_q_norm_g, gqa_k_norm_g, conv_dw_w, conv_dw_b, conv_ln_g, conv_ln_b, mix_out_g, w_out, norm2_g,
           router_w, exp_w1, exp_w3, exp_w2, final_g):
    B, N, D = x.shape
    n_ctx = ctx.shape[1]
    depth = mod_w.shape[0]
    assert n_ctx == TM == TQ and N % TM == 0
    stacked = {"norm1_g": norm1_g, "w_in": w_in, "mla_q_norm_g": mla_q_norm_g, "mla_w_uq": mla_w_uq,
               "mla_kv_norm_g": mla_kv_norm_g, "mla_w_ukv": mla_w_ukv, "gqa_q_norm_g": gqa_q_norm_g,
               "gqa_k_norm_g": gqa_k_norm_g, "conv_dw_w": conv_dw_w, "conv_dw_b": conv_dw_b,
               "conv_ln_g": conv_ln_g, "conv_ln_b": conv_ln_b, "mix_out_g": mix_out_g, "w_out": w_out,
               "norm2_g": norm2_g, "router_w": router_w}

    assert B <= CTX_ROW
    cvec = jnp.zeros((MOD_ROWS, D), F32).at[:B].set(c).at[CTX_ROW].set(c_ctx)
    mods = _modulation(cvec, mod_w, mod_b).reshape(depth, MOD_ROWS, 6, D)
    tabs = _rope_tables(N, n_ctx)

    xs = jnp.concatenate([ctx, x], axis=1)
    moe, modp = None, None
    for l in range(depth):
        lw = _prep_layer({k: v[l] for k, v in stacked.items()})
        last = l == depth - 1
        tile_off = 1 if last else 0
        xs, (q, k, v, u) = _project(xs, moe, modp, mods[l], lw, tabs)
        o = _attention(q, k, v, n_ctx, tile_off)
        xs, h, aff = _mix(o, u, xs, mods[l], lw, tile_off)
        moe = _moe_layer(aff, h, n_ctx, not last, exp_w1, exp_w3, exp_w2, l)
        modp = mods[l]
    return _final(xs, moe, modp, final_g.reshape(1, D))
```

```python
import functools
import math

import jax
import jax.numpy as jnp
from jax import lax
from jax.experimental import pallas as pl
from jax.experimental.pallas import tpu as pltpu

F32 = jnp.float32
BF16 = jnp.bfloat16
HIGHEST = lax.Precision.HIGHEST

GRID_W = 64
ROPE_BASE = 10000.0
EPS = 1e-6

MLA_HEADS = 8
MLA_Q_RANK = 256
MLA_KV_RANK = 128
MLA_NOPE = 64
MLA_ROPE = 32
MLA_V = 64
GQA_HEADS = 4
GQA_KV_HEADS = 2
GQA_HEAD_DIM = 64
CONV_CH = 256
CONV_WIDTH = 31
N_EXPERTS = 16
CAPACITY_FACTOR = 2

MLA_SCALE = 1.0 / math.sqrt(MLA_NOPE + MLA_ROPE)
GQA_SCALE = 1.0 / math.sqrt(GQA_HEAD_DIM)

LANES = 128
SUBLANES = 8
N_QHEADS = MLA_HEADS + GQA_HEADS
N_KHEADS = MLA_HEADS + GQA_KV_HEADS
N_PAIRS = N_QHEADS // 2
TM = 256
TQ = 256
TK_MAX = 2816
ONES_LANE = (MLA_V, 0)
LOG2E = math.log2(math.e)
HALO = 16
VMEM_LIMIT = 56 * 1024 * 1024
MOD_ROWS = 8
CTX_ROW = MOD_ROWS - 1

C_CQ = 0
C_CKV = C_CQ + MLA_Q_RANK
C_ZQ = C_CKV + MLA_KV_RANK
C_ZK = C_ZQ + GQA_HEADS * LANES
C_ZV = C_ZK + GQA_KV_HEADS * LANES
C_ZA = C_ZV + GQA_HEADS * LANES
C_ZB = C_ZA + CONV_CH
C_KR = C_ZB + CONV_CH
IN_PAD = C_KR + LANES


def _cparams(sem):
    return pltpu.CompilerParams(dimension_semantics=sem, vmem_limit_bytes=VMEM_LIMIT)


def _rms(x, g):
    return x * lax.rsqrt(jnp.mean(x * x, axis=-1, keepdims=True) + EPS) * g


def _head_rms(x, g, width):
    return x * lax.rsqrt(jnp.sum(x * x, axis=-1, keepdims=True) * (1.0 / width) + EPS) * g


def _rope(x, cos, sin, lo, half):
    lane = lax.broadcasted_iota(jnp.int32, x.shape, 1)
    first = (lane >= lo) & (lane < lo + half)
    rot = jnp.where(first, pltpu.roll(x, LANES - half, 1), pltpu.roll(x, half, 1))
    return x * cos + rot * sin


def _ones_col(h):
    lane = lax.broadcasted_iota(jnp.int32, (1, LANES), 1)
    return (lane == ONES_LANE[h % 2]).astype(F32)


def _mod_kernel(c_ref, w_ref, b_ref, o_ref):
    c = c_ref[...]
    a = c * jax.nn.sigmoid(c)
    o_ref[...] = jnp.dot(a, w_ref[...], preferred_element_type=F32, precision=HIGHEST) + b_ref[...]


def _modulation(cvec, mod_w, mod_b):
    L, D, W = mod_w.shape
    tn = 1536
    return pl.pallas_call(
        _mod_kernel,
        grid=(L, W // tn),
        in_specs=[pl.BlockSpec((MOD_ROWS, D), lambda l, n: (0, 0)),
                  pl.BlockSpec((None, D, tn), lambda l, n: (l, 0, n)),
                  pl.BlockSpec((None, 1, tn), lambda l, n: (l, 0, n))],
        out_specs=pl.BlockSpec((None, MOD_ROWS, tn), lambda l, n: (l, 0, n)),
        out_shape=jax.ShapeDtypeStruct((L, MOD_ROWS, W), F32),
        compiler_params=_cparams(("arbitrary", "arbitrary")),
        name="modulation",
    )(cvec, mod_w, mod_b.reshape(L, 1, W))


def _proj_kernel(*refs, has_moe):
    if has_moe:
        x_ref, moe_ref, modp_ref, refs = refs[0], refs[1], refs[2], refs[3:]
    else:
        x_ref, refs = refs[0], refs[1:]
    (mod_ref, n1_ref, win_ref, gq_ref, wuq_ref, gkv_ref, wukv_ref, ggq_ref, ggk_ref,
     cm_ref, sm_ref, cg_ref, sg_ref) = refs[:13]
    outs = refs[13:]
    if has_moe:
        xo_ref, outs = outs[0], outs[1:]
    q_ref, k_ref, v_ref, u_ref = outs

    x = x_ref[...]
    if has_moe:
        x = x + modp_ref[5:6, :] * moe_ref[...]
        xo_ref[...] = x
    hm = _rms(x, n1_ref[...]) * (1.0 + mod_ref[1:2, :]) + mod_ref[0:1, :]
    z = jnp.dot(hm.astype(BF16), win_ref[...], preferred_element_type=F32)

    cos_m, sin_m, cos_g, sin_g = cm_ref[...], sm_ref[...], cg_ref[...], sg_ref[...]

    cqn = _rms(z[:, C_CQ:C_CQ + MLA_Q_RANK], gq_ref[...])
    q = jnp.dot(cqn.astype(BF16), wuq_ref[...], preferred_element_type=F32)
    for h in range(MLA_HEADS):
        q_ref[h] = _rope(q[:, h * LANES:(h + 1) * LANES], cos_m, sin_m, MLA_NOPE, MLA_ROPE // 2).astype(BF16)

    ckvn = _rms(z[:, C_CKV:C_CKV + MLA_KV_RANK], gkv_ref[...])
    kv = jnp.dot(ckvn.astype(BF16), wukv_ref[...], preferred_element_type=F32)
    kr = _rope(z[:, C_KR:C_KR + LANES], cos_m, sin_m, MLA_NOPE, MLA_ROPE // 2)
    for h in range(MLA_HEADS):
        k_ref[h] = (kv[:, h * LANES:(h + 1) * LANES] + kr).astype(BF16)
        v_ref[h] = (kv[:, (MLA_HEADS + h) * LANES:(MLA_HEADS + h + 1) * LANES] + _ones_col(h)).astype(BF16)

    for h in range(GQA_HEADS):
        zq = z[:, C_ZQ + h * LANES:C_ZQ + (h + 1) * LANES]
        q_ref[MLA_HEADS + h] = _rope(_head_rms(zq, ggq_ref[...], GQA_HEAD_DIM), cos_g, sin_g,
                                     0, GQA_HEAD_DIM // 2).astype(BF16)
        v_ref[MLA_HEADS + h] = (z[:, C_ZV + h * LANES:C_ZV + (h + 1) * LANES] + _ones_col(h)).astype(BF16)
    for h in range(GQA_KV_HEADS):
        zk = z[:, C_ZK + h * LANES:C_ZK + (h + 1) * LANES]
        k_ref[MLA_HEADS + h] = _rope(_head_rms(zk, ggk_ref[...], GQA_HEAD_DIM), cos_g, sin_g,
                                     0, GQA_HEAD_DIM // 2).astype(BF16)

    u_ref[...] = z[:, C_ZA:C_ZA + CONV_CH] * jax.nn.sigmoid(z[:, C_ZB:C_ZB + CONV_CH])


def _mod_row(b, j):
    return jnp.where(j == 0, CTX_ROW, b)


def _project(xs, moe, modp, mod, lw, tabs):
    B, S, D = xs.shape
    nt = S // TM
    has_moe = moe is not None
    tok = lambda b, j: (b, j, 0)
    const2 = lambda b, j: (0, 0)
    modspec = pl.BlockSpec((None, 6, D), lambda b, j: (_mod_row(b, j), 0, 0))
    in_specs = [pl.BlockSpec((None, TM, D), tok)]
    args = [xs]
    if has_moe:
        in_specs += [pl.BlockSpec((None, TM, D), tok), modspec]
        args += [moe, modp]
    weights = [lw["norm1_g"], lw["w_in"], lw["mla_q_norm_g"], lw["w_uq"], lw["mla_kv_norm_g"], lw["w_ukv"],
               lw["gqa_q_norm_g"], lw["gqa_k_norm_g"]]
    in_specs += [modspec] + [pl.BlockSpec(w.shape, const2) for w in weights]
    in_specs += [pl.BlockSpec((TM, LANES), lambda b, j: (j, 0))] * 4
    args += [mod] + weights + list(tabs)
    head_spec = lambda n: pl.BlockSpec((None, n, TM, LANES), lambda b, j: (b, 0, j, 0))
    out_specs = [head_spec(N_QHEADS), head_spec(N_KHEADS), head_spec(N_QHEADS),
                 pl.BlockSpec((None, TM, CONV_CH), tok)]
    out_shape = [jax.ShapeDtypeStruct((B, N_QHEADS, S, LANES), BF16),
                 jax.ShapeDtypeStruct((B, N_KHEADS, S, LANES), BF16),
                 jax.ShapeDtypeStruct((B, N_QHEADS, S, LANES), BF16),
                 jax.ShapeDtypeStruct((B, S, CONV_CH), F32)]
    if has_moe:
        out_specs = [pl.BlockSpec((None, TM, D), tok)] + out_specs
        out_shape = [jax.ShapeDtypeStruct((B, S, D), F32)] + out_shape
    outs = pl.pallas_call(
        functools.partial(_proj_kernel, has_moe=has_moe),
        grid=(B, nt), in_specs=in_specs, out_specs=out_specs, out_shape=out_shape,
        compiler_params=_cparams(("parallel", "parallel")),
        name="project",
    )(*args)
    if has_moe:
        return outs[0], outs[1:]
    return xs, outs


def _scores(q, k):
    return lax.dot_general(q, k, (((1,), (1,)), ((), ())), preferred_element_type=F32)


def _softmax_update(s, v, m, acc, c):
    m_new = jnp.maximum(m, jnp.max(s, axis=-1, keepdims=True))
    alpha = jnp.exp2((m - m_new) * c)
    p = jnp.exp2((s - m_new) * c).astype(BF16)
    acc = alpha * acc + jnp.dot(p, v, preferred_element_type=F32)
    return m_new, acc


def _attn_kernel(q_ref, k0_ref, k1_ref, v_ref, o_ref, s_ref, *, tile_off, n_ctx, n_all, tk):
    hp = pl.program_id(1)
    j = pl.program_id(2) + tile_off
    c = jnp.where(hp < MLA_HEADS // 2, MLA_SCALE * LOG2E, GQA_SCALE * LOG2E).astype(F32)
    q0, q1 = q_ref[0], q_ref[1]
    init = (jnp.full((TQ, 1), -jnp.inf, F32), jnp.zeros((TQ, LANES), F32))

    def finish(acc0, acc1):
        lane = lax.broadcasted_iota(jnp.int32, (TQ, LANES), 1)
        l0 = acc0[:, ONES_LANE[0]:ONES_LANE[0] + 1]
        l1 = acc1[:, ONES_LANE[1]:ONES_LANE[1] + 1]
        o_ref[...] = jnp.where(lane < LANES // 2, acc0 / l0, acc1 / l1)

    @pl.when(j == 0)
    def _():
        _, acc0 = _softmax_update(_scores(q0, k0_ref[0:n_ctx, :]), v_ref[0, 0:n_ctx, :], *init, c)
        _, acc1 = _softmax_update(_scores(q1, k1_ref[0:n_ctx, :]), v_ref[1, 0:n_ctx, :], *init, c)
        finish(acc0, acc1)

    @pl.when(j > 0)
    def _():
        n = n_all // tk

        def put(slot, r):
            s_ref[slot, 0] = _scores(q0, k0_ref[pl.ds(r, tk), :])
            s_ref[slot, 1] = _scores(q1, k1_ref[pl.ds(r, tk), :])

        def step(slot, r, carry):
            m0, a0, m1, a1 = carry
            m0, a0 = _softmax_update(s_ref[slot, 0], v_ref[0, pl.ds(r, tk), :], m0, a0, c)
            m1, a1 = _softmax_update(s_ref[slot, 1], v_ref[1, pl.ds(r, tk), :], m1, a1, c)
            return m0, a0, m1, a1

        def body(t, carry):
            r0 = pl.multiple_of(2 * t * tk, tk)
            r1 = pl.multiple_of(r0 + tk, tk)
            r2 = pl.multiple_of(r1 + tk, tk)
            put(1, r1)
            carry = step(0, r0, carry)
            put(0, r2)
            return step(1, r1, carry)

        put(0, 0)
        paired = (n - 1) // 2
        carry = lax.fori_loop(0, paired, body, init + init)
        for t in range(2 * paired, n):
            if t + 1 < n:
                put((t + 1) % 2, (t + 1) * tk)
            carry = step(t % 2, t * tk, carry)
        finish(carry[1], carry[3])


def _kv_head(h):
    group = GQA_HEADS // GQA_KV_HEADS
    return jnp.where(h < MLA_HEADS, h, MLA_HEADS + (h - MLA_HEADS) // group)


def _attention(q, k, v, n_ctx, tile_off):
    B, _, S, _ = q.shape
    nq = S // TQ - tile_off
    tk = next(t for t in range(TK_MAX, 0, -LANES) if S % t == 0)
    kspec = lambda i: pl.BlockSpec((None, None, S, LANES), lambda b, hp, j: (b, _kv_head(2 * hp + i), 0, 0))
    return pl.pallas_call(
        functools.partial(_attn_kernel, tile_off=tile_off, n_ctx=n_ctx, n_all=S, tk=tk),
        grid=(B, N_PAIRS, nq),
        in_specs=[pl.BlockSpec((None, 2, TQ, LANES), lambda b, hp, j: (b, hp, j + tile_off, 0)),
                  kspec(0), kspec(1),
                  pl.BlockSpec((None, 2, S, LANES), lambda b, hp, j: (b, hp, 0, 0))],
        out_specs=pl.BlockSpec((None, TQ, LANES), lambda b, hp, j: (b, j, hp)),
        out_shape=jax.ShapeDtypeStruct((B, nq * TQ, N_PAIRS * LANES), F32),
        scratch_shapes=[pltpu.VMEM((2, 2, TQ, tk), F32)],
        compiler_params=_cparams(("parallel", "parallel", "arbitrary")),
        name="attention",
    )(q, k, k, v)


def _mix_kernel(o_ref, up_ref, uc_ref, un_ref, x_ref, mod_ref, cw_ref, cb_ref, lg_ref, lb_ref,
                mg_ref, wout_ref, n2_ref, rw_ref,
                xo_ref, h_ref, aff_ref, ext_ref, *, tile_off, last_tile):
    j = pl.program_id(1) + tile_off
    has_prev = j >= 2
    has_next = (j >= 1) & (j < last_tile)
    ext_ref[0:HALO, :] = jnp.where(has_prev, up_ref[TM - HALO:TM, :], 0.0)
    ext_ref[HALO:HALO + TM, :] = uc_ref[...]
    ext_ref[HALO + TM:HALO + TM + HALO, :] = jnp.where(has_next, un_ref[0:HALO, :], 0.0)
    ext = ext_ref[...]
    r0 = HALO - CONV_WIDTH // 2
    y = jnp.zeros((TM, CONV_CH), F32) + cb_ref[...]
    for phase in range(SUBLANES):
        shifted = ext if phase == 0 else pltpu.roll(ext, TM + 2 * HALO - phase, 0)
        for r in range(phase, r0 + CONV_WIDTH, SUBLANES):
            if r >= r0:
                y = y + shifted[r - phase:r - phase + TM, :] * cw_ref[r - r0:r - r0 + 1, :]
    mu = jnp.mean(y, axis=-1, keepdims=True)
    yc = y - mu
    yn = yc * lax.rsqrt(jnp.mean(yc * yc, axis=-1, keepdims=True) + EPS) * lg_ref[...] + lb_ref[...]
    o_conv = yn * jax.nn.sigmoid(yn)

    o = o_ref[...]
    w_mla = MLA_HEADS * MLA_V
    w_att = w_mla + GQA_HEADS * GQA_HEAD_DIM
    cat = jnp.concatenate([
        _rms(o[:, :w_mla], mg_ref[:, :w_mla]).astype(BF16),
        _rms(o[:, w_mla:w_att], mg_ref[:, w_mla:w_att]).astype(BF16),
        _rms(o_conv, mg_ref[:, w_att:]).astype(BF16)], axis=-1)
    mix = jnp.dot(cat, wout_ref[...], preferred_element_type=F32)
    x = x_ref[...] + mod_ref[2:3, :] * mix
    xo_ref[...] = x
    hm = _rms(x, n2_ref[...]) * (1.0 + mod_ref[4:5, :]) + mod_ref[3:4, :]
    h_ref[...] = hm
    logits = lax.dot_general(rw_ref[...], hm, (((1,), (1,)), ((), ())),
                             preferred_element_type=F32, precision=HIGHEST)
    e = jnp.exp(logits - jnp.max(logits, axis=0, keepdims=True))
    aff_ref[...] = e / jnp.sum(e, axis=0, keepdims=True)


def _mix(o, u, xs, mod, lw, tile_off):
    B, S_o, _ = o.shape
    D = xs.shape[-1]
    nt_all = u.shape[1] // TM
    nt = S_o // TM
    const2 = lambda b, j: (0, 0)
    weights = [lw["conv_w"], lw["conv_dw_b"], lw["conv_ln_g"], lw["conv_ln_b"], lw["mix_out_g"], lw["w_out"],
               lw["norm2_g"], lw["router_wt"]]
    uspec = lambda d: pl.BlockSpec(
        (None, TM, CONV_CH), lambda b, j: (b, jnp.clip(j + tile_off + d, 0, nt_all - 1), 0))
    return pl.pallas_call(
        functools.partial(_mix_kernel, tile_off=tile_off, last_tile=nt_all - 1),
        grid=(B, nt),
        in_specs=[pl.BlockSpec((None, TM, o.shape[-1]), lambda b, j: (b, j, 0)),
                  uspec(-1), uspec(0), uspec(1),
                  pl.BlockSpec((None, TM, D), lambda b, j: (b, j + tile_off, 0)),
                  pl.BlockSpec((None, 6, D), lambda b, j: (_mod_row(b, j + tile_off), 0, 0))]
                 + [pl.BlockSpec(w.shape, const2) for w in weights],
        out_specs=[pl.BlockSpec((None, TM, D), lambda b, j: (b, j, 0)),
                   pl.BlockSpec((None, TM, D), lambda b, j: (b, j, 0)),
                   pl.BlockSpec((None, N_EXPERTS, TM), lambda b, j: (b, 0, j))],
        out_shape=[jax.ShapeDtypeStruct((B, S_o, D), F32),
                   jax.ShapeDtypeStruct((B, S_o, D), F32),
                   jax.ShapeDtypeStruct((B, N_EXPERTS, S_o), F32)],
        scratch_shapes=[pltpu.VMEM((TM + 2 * HALO, CONV_CH), F32)],
        compiler_params=_cparams(("parallel", "parallel")),
        name="mix",
    )(o, u, u, u, xs, mod, *weights)


def _moe_kernel(idx_ref, h_hbm, g_ref, w1_ref, w3_ref, w2_ref, y_ref, xbuf0, xbuf1, w1b, w3b, w2b, sem,
                *, nchunk, rows, per_it, r_pad):
    e, f = pl.program_id(0), pl.program_id(1)
    ne, nf = pl.num_programs(0), pl.num_programs(1)
    xbufs = (xbuf0, xbuf1)

    def request(expert, r0, n, into):
        for k in range(n):
            src = idx_ref[expert * r_pad + r0 + k]
            pltpu.make_async_copy(h_hbm.at[pl.ds(src, 1)], xbufs[into].at[pl.ds(r0 + k, 1)], sem.at[into]).start()

    def wait_rows(which):
        pltpu.make_async_copy(h_hbm.at[pl.ds(0, r_pad)], xbufs[which], sem.at[which]).wait()

    @pl.when((e == 0) & (f == 0))
    def _():
        def first(i, carry):
            request(0, pl.multiple_of(i * per_it, 8), per_it, 0)
            return carry
        lax.fori_loop(0, r_pad // per_it, first, 0)

    nxt = jnp.minimum(e + 1, ne - 1)

    def run(cur):
        @pl.when(f == 0)
        def _():
            wait_rows(cur)
            y_ref[...] = jnp.zeros_like(y_ref)

        w1b[...] = w1_ref[...].astype(BF16)
        w3b[...] = w3_ref[...].astype(BF16)
        w2b[...] = w2_ref[...].astype(BF16)
        for c in range(nchunk):
            request(nxt, pl.multiple_of((f * nchunk + c) * per_it, 8), per_it, 1 - cur)
            xb = xbufs[cur][c * rows:(c + 1) * rows, :].astype(BF16)
            h1 = jnp.dot(xb, w1b[...], preferred_element_type=F32)
            h3 = jnp.dot(xb, w3b[...], preferred_element_type=F32)
            hid = (h1 * jax.nn.sigmoid(h1) * h3).astype(BF16)
            y_ref[c * rows:(c + 1) * rows, :] += jnp.dot(hid, w2b[...], preferred_element_type=F32)

        @pl.when((e == ne - 1) & (f == nf - 1))
        def _():
            wait_rows(1 - cur)

    for parity in (0, 1):
        pl.when(e % 2 == parity)(functools.partial(run, parity))

    @pl.when(f == nf - 1)
    def _():
        y_ref[...] = y_ref[...] * g_ref[...]


def _moe_tiles(R, FF):
    rows = next(r for r in range(1056, 15, -16) if R % r == 0)
    tf = next(t for t in range(512, LANES - 1, -LANES) if FF % t == 0)
    steps = (R // rows) * (FF // tf)
    per_it = 8 * (-(-R // (8 * steps)))
    return rows, tf, per_it, per_it * steps


def _experts(h_flat, gidx, gate, w1, w3, w2, l):
    E, R = gidx.shape
    D = h_flat.shape[-1]
    FF = w1.shape[-1]
    rows, tf, per_it, r_pad = _moe_tiles(R, FF)
    idx = jnp.pad(gidx, ((0, 0), (0, r_pad - R))).reshape(E * r_pad)
    grid_spec = pltpu.PrefetchScalarGridSpec(
        num_scalar_prefetch=1,
        grid=(E, FF // tf),
        in_specs=[pl.BlockSpec(memory_space=pl.ANY),
                  pl.BlockSpec((None, R, 1), lambda e, f, idx: (e, 0, 0)),
                  pl.BlockSpec((None, None, D, tf), lambda e, f, idx: (l, e, 0, f)),
                  pl.BlockSpec((None, None, D, tf), lambda e, f, idx: (l, e, 0, f)),
                  pl.BlockSpec((None, None, tf, D), lambda e, f, idx: (l, e, f, 0))],
        out_specs=pl.BlockSpec((None, R, D), lambda e, f, idx: (e, 0, 0)),
        scratch_shapes=[pltpu.VMEM((r_pad, D), F32), pltpu.VMEM((r_pad, D), F32),
                        pltpu.VMEM((D, tf), BF16), pltpu.VMEM((D, tf), BF16), pltpu.VMEM((tf, D), BF16),
                        pltpu.SemaphoreType.DMA((2,))])
    return pl.pallas_call(
        functools.partial(_moe_kernel, nchunk=R // rows, rows=rows, per_it=per_it, r_pad=r_pad),
        grid_spec=grid_spec,
        out_shape=jax.ShapeDtypeStruct((E, R, D), F32),
        compiler_params=_cparams(("arbitrary", "arbitrary")),
        name="experts",
    )(idx, h_flat, gate, w1, w3, w2)


COMBINE_CHUNK = 256


def _combine_kernel(src_ref, dst_ref, y_hbm, out_hbm, acc, stage0, stage1, sem, osem, *, n_entries):
    b = pl.program_id(0)
    stages = (stage0, stage1)
    nchunks = n_entries // COMBINE_CHUNK
    acc[...] = jnp.zeros_like(acc)

    def request(c, row, slot):
        src = src_ref[b * n_entries + c * COMBINE_CHUNK + row]
        pltpu.make_async_copy(y_hbm.at[pl.ds(src, 1)], stages[slot].at[pl.ds(row, 1)], sem.at[slot]).start()

    def wait_chunk(slot):
        pltpu.make_async_copy(y_hbm.at[pl.ds(0, COMBINE_CHUNK)], stages[slot], sem.at[slot]).wait()

    def first(i, carry):
        for k in range(SUBLANES):
            request(0, pl.multiple_of(i * SUBLANES, SUBLANES) + k, 0)
        return carry
    lax.fori_loop(0, COMBINE_CHUNK // SUBLANES, first, 0)

    def add_chunk(c, slot):
        wait_chunk(slot)
        nxt = jnp.minimum(c + 1, nchunks - 1)
        base = b * n_entries + c * COMBINE_CHUNK

        def group(i, carry):
            r0 = pl.multiple_of(i * SUBLANES, SUBLANES)
            for k in range(SUBLANES):
                request(nxt, r0 + k, 1 - slot)
            for k in range(SUBLANES):
                dst = dst_ref[base + r0 + k]
                acc[pl.ds(dst, 1), :] += stages[slot][pl.ds(r0 + k, 1), :]
            return carry
        lax.fori_loop(0, COMBINE_CHUNK // SUBLANES, group, 0)

    def pair(t, carry):
        add_chunk(2 * t, 0)
        add_chunk(2 * t + 1, 1)
        return carry
    lax.fori_loop(0, nchunks // 2, pair, 0)
    wait_chunk(0)
    done = pltpu.make_async_copy(acc, out_hbm.at[b], osem)
    done.start()
    done.wait()


def _combine(y_flat, src, dst, B, S_o):
    D = y_flat.shape[-1]
    n_entries = src.shape[0] // B
    assert n_entries % (2 * COMBINE_CHUNK) == 0
    grid_spec = pltpu.PrefetchScalarGridSpec(
        num_scalar_prefetch=2,
        grid=(B,),
        in_specs=[pl.BlockSpec(memory_space=pl.ANY)],
        out_specs=pl.BlockSpec(memory_space=pl.ANY),
        scratch_shapes=[pltpu.VMEM((S_o, D), F32),
                        pltpu.VMEM((COMBINE_CHUNK, D), F32), pltpu.VMEM((COMBINE_CHUNK, D), F32),
                        pltpu.SemaphoreType.DMA((2,)), pltpu.SemaphoreType.DMA])
    return pl.pallas_call(
        functools.partial(_combine_kernel, n_entries=n_entries),
        grid_spec=grid_spec,
        out_shape=jax.ShapeDtypeStruct((B, S_o, D), F32),
        compiler_params=_cparams(("arbitrary",)),
        name="combine",
    )(src, dst, y_flat)


def _final_kernel(x_ref, moe_ref, mod_ref, g_ref, o_ref):
    x = x_ref[...] + mod_ref[5:6, :] * moe_ref[...]
    o_ref[...] = _rms(x, g_ref[...])


def _final(x, moe, mod, g):
    B, N, D = x.shape
    tok = lambda b, j: (b, j, 0)
    return pl.pallas_call(
        _final_kernel,
        grid=(B, N // TM),
        in_specs=[pl.BlockSpec((None, TM, D), tok), pl.BlockSpec((None, TM, D), tok),
                  pl.BlockSpec((None, 6, D), lambda b, j: (b, 0, 0)),
                  pl.BlockSpec((1, D), lambda b, j: (0, 0))],
        out_specs=pl.BlockSpec((None, TM, D), tok),
        out_shape=jax.ShapeDtypeStruct((B, N, D), F32),
        compiler_params=_cparams(("parallel", "parallel")),
        name="final_norm",
    )(x, moe, mod, g)


def _pad_cols(w, lo, width=LANES):
    return jnp.pad(w, ((0, 0), (lo, width - lo - w.shape[1])))


def _prep_layer(p):
    w_in = p["w_in"]
    o = 0
    parts = {}
    for name, size in (("cq", MLA_Q_RANK), ("ckv", MLA_KV_RANK), ("kr", MLA_ROPE),
                       ("zq", GQA_HEADS * GQA_HEAD_DIM), ("zk", GQA_KV_HEADS * GQA_HEAD_DIM),
                       ("zv", GQA_KV_HEADS * GQA_HEAD_DIM), ("zc", 2 * CONV_CH)):
        parts[name] = w_in[:, o:o + size]
        o += size
    hd = GQA_HEAD_DIM
    group = GQA_HEADS // GQA_KV_HEADS
    cols = [parts["cq"], parts["ckv"]]
    cols += [_pad_cols(parts["zq"][:, h * hd:(h + 1) * hd], 0) for h in range(GQA_HEADS)]
    cols += [_pad_cols(parts["zk"][:, h * hd:(h + 1) * hd], 0) for h in range(GQA_KV_HEADS)]
    cols += [_pad_cols(parts["zv"][:, (h // group) * hd:(h // group + 1) * hd], (h % 2) * hd)
             for h in range(GQA_HEADS)]
    cols += [parts["zc"], _pad_cols(parts["kr"], MLA_NOPE)]
    w_in_p = jnp.concatenate(cols, axis=1).astype(BF16)

    dq = MLA_NOPE + MLA_ROPE
    w_uq = jnp.concatenate([_pad_cols(p["mla_w_uq"][:, h * dq:(h + 1) * dq], 0) for h in range(MLA_HEADS)],
                           axis=1).astype(BF16)
    dkv = MLA_NOPE + MLA_V
    wk = [_pad_cols(p["mla_w_ukv"][:, h * dkv:h * dkv + MLA_NOPE], 0) for h in range(MLA_HEADS)]
    wv = [_pad_cols(p["mla_w_ukv"][:, h * dkv + MLA_NOPE:(h + 1) * dkv], (h % 2) * MLA_V) for h in range(MLA_HEADS)]
    w_ukv = jnp.concatenate(wk + wv, axis=1).astype(BF16)

    row = lambda a: a.reshape(1, -1)
    return {
        "norm1_g": row(p["norm1_g"]), "w_in": w_in_p,
        "mla_q_norm_g": row(p["mla_q_norm_g"]), "w_uq": w_uq,
        "mla_kv_norm_g": row(p["mla_kv_norm_g"]), "w_ukv": w_ukv,
        "gqa_q_norm_g": _pad_cols(row(p["gqa_q_norm_g"]), 0), "gqa_k_norm_g": _pad_cols(row(p["gqa_k_norm_g"]), 0),
        "conv_w": jnp.pad(p["conv_dw_w"], ((0, 32 - CONV_WIDTH), (0, 0))),
        "conv_dw_b": row(p["conv_dw_b"]), "conv_ln_g": row(p["conv_ln_g"]), "conv_ln_b": row(p["conv_ln_b"]),
        "mix_out_g": row(p["mix_out_g"]), "w_out": p["w_out"].astype(BF16),
        "norm2_g": row(p["norm2_g"]), "router_wt": p["router_w"].T,
    }


def _rope_tables(n, n_ctx):
    rows = n // GRID_W
    row = jnp.repeat(jnp.arange(rows), GRID_W).astype(F32)
    col = jnp.tile(jnp.arange(GRID_W), rows).astype(F32)
    tabs = []
    for d_rot, lo in ((MLA_ROPE, MLA_NOPE), (GQA_HEAD_DIM, 0)):
        d_axis = d_rot // 2
        inv = ROPE_BASE ** (-jnp.arange(0, d_axis, 2, dtype=F32) / d_axis)
        ang = jnp.concatenate([row[:, None] * inv, col[:, None] * inv], axis=-1)
        cos, sin = jnp.cos(ang), jnp.sin(ang)
        cos_b = jnp.pad(jnp.concatenate([cos, cos], axis=1) - 1.0, ((n_ctx, 0), (lo, LANES - lo - d_rot))) + 1.0
        sin_b = jnp.pad(jnp.concatenate([-sin, sin], axis=1), ((n_ctx, 0), (lo, LANES - lo - d_rot)))
        tabs += [cos_b, sin_b]
    return tabs


def _route(aff, cap):
    gate, idx = lax.top_k(aff, cap)
    return idx.transpose(1, 0, 2), gate.transpose(1, 0, 2)


def _moe_layer(aff, h, n_ctx, with_ctx, w1, w3, w2, l):
    B, S_o, D = h.shape
    E = aff.shape[1]
    lat0 = n_ctx if with_ctx else 0
    sets = [(lat0, S_o - lat0)] + ([(0, n_ctx)] if with_ctx else [])
    tok, gates = [], []
    for row0, n in sets:
        idx, gate = _route(aff[:, :, row0:row0 + n], CAPACITY_FACTOR * n // N_EXPERTS)
        tok.append(idx + row0)
        gates.append(gate)
    caps = [t.shape[-1] for t in tok]
    R = B * sum(caps)
    sample = jnp.arange(B, dtype=jnp.int32)[None, :, None]
    gidx = jnp.concatenate([(t + sample * S_o).reshape(E, -1) for t in tok], axis=1)
    gate = jnp.concatenate([g.reshape(E, -1) for g in gates], axis=1)
    y = _experts(h.reshape(B * S_o, D), gidx, gate[..., None], w1, w3, w2, l)
    expert = jnp.arange(E, dtype=jnp.int32)[:, None, None]
    offs = [B * sum(caps[:i]) for i in range(len(caps))]
    src = [expert * R + off + sample * cap + jnp.arange(cap, dtype=jnp.int32) for off, cap in zip(offs, caps)]
    src = jnp.concatenate([jnp.broadcast_to(s_, t.shape) for s_, t in zip(src, tok)], axis=2)
    dst = jnp.concatenate(tok, axis=2)
    to_sample_major = lambda a: a.transpose(1, 0, 2).reshape(-1).astype(jnp.int32)
    return _combine(y.reshape(E * R, D), to_sample_major(src), to_sample_major(dst), B, S_o)


def kernel(x, c, ctx, c_ctx, mod_w, mod_b, norm1_g, w_in, mla_q_norm_g, mla_w_uq, mla_kv_norm_g, mla_w_ukv,
           gqa_q_norm_g, gqa_k_norm_g, conv_dw_w, conv_dw_b, conv_ln_g, conv_ln_b, mix_out_g, w_out, norm2_g,
           router_w, exp_w1, exp_w3, exp_w2, final_g):
    B, N, D = x.shape
    n_ctx = ctx.shape[1]
    depth = mod_w.shape[0]
    assert n_ctx == TM == TQ and N % TM == 0
    stacked = {"norm1_g": norm1_g, "w_in": w_in, "mla_q_norm_g": mla_q_norm_g, "mla_w_uq": mla_w_uq,
               "mla_kv_norm_g": mla_kv_norm_g, "mla_w_ukv": mla_w_ukv, "gqa_q_norm_g": gqa_q_norm_g,
               "gqa_k_norm_g": gqa_k_norm_g, "conv_dw_w": conv_dw_w, "conv_dw_b": conv_dw_b,
               "conv_ln_g": conv_ln_g, "conv_ln_b": conv_ln_b, "mix_out_g": mix_out_g, "w_out": w_out,
               "norm2_g": norm2_g, "router_w": router_w}

    assert B <= CTX_ROW
    cvec = jnp.zeros((MOD_ROWS, D), F32).at[:B].set(c).at[CTX_ROW].set(c_ctx)
    mods = _modulation(cvec, mod_w, mod_b).reshape(depth, MOD_ROWS, 6, D)
    tabs = _rope_tables(N, n_ctx)

    xs = jnp.concatenate([ctx, x], axis=1)
    moe, modp = None, None
    for l in range(depth):
        lw = _prep_layer({k: v[l] for k, v in stacked.items()})
        last = l == depth - 1
        tile_off = 1 if last else 0
        xs, (q, k, v, u) = _project(xs, moe, modp, mods[l], lw, tabs)
        o = _attention(q, k, v, n_ctx, tile_off)
        xs, h, aff = _mix(o, u, xs, mods[l], lw, tile_off)
        moe = _moe_layer(aff, h, n_ctx, not last, exp_w1, exp_w3, exp_w2, l)
        modp = mods[l]
    return _final(xs, moe, modp, final_g.reshape(1, D))
```

```python
import functools
import math

import jax
import jax.numpy as jnp
from jax import lax
from jax.experimental import pallas as pl
from jax.experimental.pallas import tpu as pltpu

F32 = jnp.float32
BF16 = jnp.bfloat16
HIGHEST = lax.Precision.HIGHEST

GRID_W = 64
ROPE_BASE = 10000.0
EPS = 1e-6

MLA_HEADS = 8
MLA_Q_RANK = 256
MLA_KV_RANK = 128
MLA_NOPE = 64
MLA_ROPE = 32
MLA_V = 64
GQA_HEADS = 4
GQA_KV_HEADS = 2
GQA_HEAD_DIM = 64
CONV_CH = 256
CONV_WIDTH = 31
N_EXPERTS = 16
CAPACITY_FACTOR = 2

MLA_SCALE = 1.0 / math.sqrt(MLA_NOPE + MLA_ROPE)
GQA_SCALE = 1.0 / math.sqrt(GQA_HEAD_DIM)

LANES = 128
SUBLANES = 8
N_QHEADS = MLA_HEADS + GQA_HEADS
N_KHEADS = MLA_HEADS + GQA_KV_HEADS
N_PAIRS = N_QHEADS // 2
TM = 256
TQ = 256
TK_MAX = 2816
ONES_LANE = (MLA_V, 0)
LOG2E = math.log2(math.e)
HALO = 16
VMEM_LIMIT = 56 * 1024 * 1024
MOD_ROWS = 8
CTX_ROW = MOD_ROWS - 1

C_CQ = 0
C_CKV = C_CQ + MLA_Q_RANK
C_ZQ = C_CKV + MLA_KV_RANK
C_ZK = C_ZQ + GQA_HEADS * LANES
C_ZV = C_ZK + GQA_KV_HEADS * LANES
C_ZA = C_ZV + GQA_HEADS * LANES
C_ZB = C_ZA + CONV_CH
C_KR = C_ZB + CONV_CH
IN_PAD = C_KR + LANES


def _cparams(sem):
    return pltpu.CompilerParams(dimension_semantics=sem, vmem_limit_bytes=VMEM_LIMIT)


def _rms(x, g):
    return x * lax.rsqrt(jnp.mean(x * x, axis=-1, keepdims=True) + EPS) * g


def _head_rms(x, g, width):
    return x * lax.rsqrt(jnp.sum(x * x, axis=-1, keepdims=True) * (1.0 / width) + EPS) * g


def _rope(x, cos, sin, lo, half):
    lane = lax.broadcasted_iota(jnp.int32, x.shape, 1)
    first = (lane >= lo) & (lane < lo + half)
    rot = jnp.where(first, pltpu.roll(x, LANES - half, 1), pltpu.roll(x, half, 1))
    return x * cos + rot * sin


def _ones_col(h):
    lane = lax.broadcasted_iota(jnp.int32, (1, LANES), 1)
    return (lane == ONES_LANE[h % 2]).astype(F32)


def _mod_kernel(c_ref, w_ref, b_ref, o_ref):
    c = c_ref[...]
    a = c * jax.nn.sigmoid(c)
    o_ref[...] = jnp.dot(a, w_ref[...], preferred_element_type=F32, precision=HIGHEST) + b_ref[...]


def _modulation(cvec, mod_w, mod_b):
    L, D, W = mod_w.shape
    tn = 1536
    return pl.pallas_call(
        _mod_kernel,
        grid=(L, W // tn),
        in_specs=[pl.BlockSpec((MOD_ROWS, D), lambda l, n: (0, 0)),
                  pl.BlockSpec((None, D, tn), lambda l, n: (l, 0, n)),
                  pl.BlockSpec((None, 1, tn), lambda l, n: (l, 0, n))],
        out_specs=pl.BlockSpec((None, MOD_ROWS, tn), lambda l, n: (l, 0, n)),
        out_shape=jax.ShapeDtypeStruct((L, MOD_ROWS, W), F32),
        compiler_params=_cparams(("arbitrary", "arbitrary")),
        name="modulation",
    )(cvec, mod_w, mod_b.reshape(L, 1, W))


def _proj_kernel(*refs, has_moe):
    if has_moe:
        x_ref, moe_ref, modp_ref, refs = refs[0], refs[1], refs[2], refs[3:]
    else:
        x_ref, refs = refs[0], refs[1:]
    (mod_ref, n1_ref, win_ref, gq_ref, wuq_ref, gkv_ref, wukv_ref, ggq_ref, ggk_ref,
     cm_ref, sm_ref, cg_ref, sg_ref) = refs[:13]
    outs = refs[13:]
    if has_moe:
        xo_ref, outs = outs[0], outs[1:]
    q_ref, k_ref, v_ref, u_ref = outs

    x = x_ref[...]
    if has_moe:
        x = x + modp_ref[5:6, :] * moe_ref[...]
        xo_ref[...] = x
    hm = _rms(x, n1_ref[...]) * (1.0 + mod_ref[1:2, :]) + mod_ref[0:1, :]
    z = jnp.dot(hm.astype(BF16), win_ref[...], preferred_element_type=F32)

    cos_m, sin_m, cos_g, sin_g = cm_ref[...], sm_ref[...], cg_ref[...], sg_ref[...]

    cqn = _rms(z[:, C_CQ:C_CQ + MLA_Q_RANK], gq_ref[...])
    q = jnp.dot(cqn.astype(BF16), wuq_ref[...], preferred_element_type=F32)
    for h in range(MLA_HEADS):
        q_ref[h] = _rope(q[:, h * LANES:(h + 1) * LANES], cos_m, sin_m, MLA_NOPE, MLA_ROPE // 2).astype(BF16)

    ckvn = _rms(z[:, C_CKV:C_CKV + MLA_KV_RANK], gkv_ref[...])
    kv = jnp.dot(ckvn.astype(BF16), wukv_ref[...], preferred_element_type=F32)
    kr = _rope(z[:, C_KR:C_KR + LANES], cos_m, sin_m, MLA_NOPE, MLA_ROPE // 2)
    for h in range(MLA_HEADS):
        k_ref[h] = (kv[:, h * LANES:(h + 1) * LANES] + kr).astype(BF16)
        v_ref[h] = (kv[:, (MLA_HEADS + h) * LANES:(MLA_HEADS + h + 1) * LANES] + _ones_col(h)).astype(BF16)

    for h in range(GQA_HEADS):
        zq = z[:, C_ZQ + h * LANES:C_ZQ + (h + 1) * LANES]
        q_ref[MLA_HEADS + h] = _rope(_head_rms(zq, ggq_ref[...], GQA_HEAD_DIM), cos_g, sin_g,
                                     0, GQA_HEAD_DIM // 2).astype(BF16)
        v_ref[MLA_HEADS + h] = (z[:, C_ZV + h * LANES:C_ZV + (h + 1) * LANES] + _ones_col(h)).astype(BF16)
    for h in range(GQA_KV_HEADS):
        zk = z[:, C_ZK + h * LANES:C_ZK + (h + 1) * LANES]
        k_ref[MLA_HEADS + h] = _rope(_head_rms(zk, ggk_ref[...], GQA_HEAD_DIM), cos_g, sin_g,
                                     0, GQA_HEAD_DIM // 2).astype(BF16)

    u_ref[...] = z[:, C_ZA:C_ZA + CONV_CH] * jax.nn.sigmoid(z[:, C_ZB:C_ZB + CONV_CH])


def _mod_row(b, j):
    return jnp.where(j == 0, CTX_ROW, b)


def _project(xs, moe, modp, mod, lw, tabs):
    B, S, D = xs.shape
    nt = S // TM
    has_moe = moe is not None
    tok = lambda b, j: (b, j, 0)
    const2 = lambda b, j: (0, 0)
    modspec = pl.BlockSpec((None, 6, D), lambda b, j: (_mod_row(b, j), 0, 0))
    in_specs = [pl.BlockSpec((None, TM, D), tok)]
    args = [xs]
    if has_moe:
        in_specs += [pl.BlockSpec((None, TM, D), tok), modspec]
        args += [moe, modp]
    weights = [lw["norm1_g"], lw["w_in"], lw["mla_q_norm_g"], lw["w_uq"], lw["mla_kv_norm_g"], lw["w_ukv"],
               lw["gqa_q_norm_g"], lw["gqa_k_norm_g"]]
    in_specs += [modspec] + [pl.BlockSpec(w.shape, const2) for w in weights]
    in_specs += [pl.BlockSpec((TM, LANES), lambda b, j: (j, 0))] * 4
    args += [mod] + weights + list(tabs)
    head_spec = lambda n: pl.BlockSpec((None, n, TM, LANES), lambda b, j: (b, 0, j, 0))
    out_specs = [head_spec(N_QHEADS), head_spec(N_KHEADS), head_spec(N_QHEADS),
                 pl.BlockSpec((None, TM, CONV_CH), tok)]
    out_shape = [jax.ShapeDtypeStruct((B, N_QHEADS, S, LANES), BF16),
                 jax.ShapeDtypeStruct((B, N_KHEADS, S, LANES), BF16),
                 jax.ShapeDtypeStruct((B, N_QHEADS, S, LANES), BF16),
                 jax.ShapeDtypeStruct((B, S, CONV_CH), F32)]
    if has_moe:
        out_specs = [pl.BlockSpec((None, TM, D), tok)] + out_specs
        out_shape = [jax.ShapeDtypeStruct((B, S, D), F32)] + out_shape
    outs = pl.pallas_call(
        functools.partial(_proj_kernel, has_moe=has_moe),
        grid=(B, nt), in_specs=in_specs, out_specs=out_specs, out_shape=out_shape,
        compiler_params=_cparams(("parallel", "parallel")),
        name="project",
    )(*args)
    if has_moe:
        return outs[0], outs[1:]
    return xs, outs


def _scores(q, k):
    return lax.dot_general(q, k, (((1,), (1,)), ((), ())), preferred_element_type=F32)


def _softmax_update(s, v, m, acc, c):
    m_new = jnp.maximum(m, jnp.max(s, axis=-1, keepdims=True))
    alpha = jnp.exp2((m - m_new) * c)
    p = jnp.exp2((s - m_new) * c).astype(BF16)
    acc = alpha * acc + jnp.dot(p, v, preferred_element_type=F32)
    return m_new, acc


def _attn_kernel(q_ref, k0_ref, k1_ref, v_ref, o_ref, s_ref, *, tile_off, n_ctx, n_all, tk):
    hp = pl.program_id(1)
    j = pl.program_id(2) + tile_off
    c = jnp.where(hp < MLA_HEADS // 2, MLA_SCALE * LOG2E, GQA_SCALE * LOG2E).astype(F32)
    q0, q1 = q_ref[0], q_ref[1]
    init = (jnp.full((TQ, 1), -jnp.inf, F32), jnp.zeros((TQ, LANES), F32))

    def finish(acc0, acc1):
        lane = lax.broadcasted_iota(jnp.int32, (TQ, LANES), 1)
        l0 = acc0[:, ONES_LANE[0]:ONES_LANE[0] + 1]
        l1 = acc1[:, ONES_LANE[1]:ONES_LANE[1] + 1]
        o_ref[...] = jnp.where(lane < LANES // 2, acc0 / l0, acc1 / l1)

    @pl.when(j == 0)
    def _():
        _, acc0 = _softmax_update(_scores(q0, k0_ref[0:n_ctx, :]), v_ref[0, 0:n_ctx, :], *init, c)
        _, acc1 = _softmax_update(_scores(q1, k1_ref[0:n_ctx, :]), v_ref[1, 0:n_ctx, :], *init, c)
        finish(acc0, acc1)

    @pl.when(j > 0)
    def _():
        n = n_all // tk

        def put(slot, r):
            s_ref[slot, 0] = _scores(q0, k0_ref[pl.ds(r, tk), :])
            s_ref[slot, 1] = _scores(q1, k1_ref[pl.ds(r, tk), :])

        def step(slot, r, carry):
            m0, a0, m1, a1 = carry
            m0, a0 = _softmax_update(s_ref[slot, 0], v_ref[0, pl.ds(r, tk), :], m0, a0, c)
            m1, a1 = _softmax_update(s_ref[slot, 1], v_ref[1, pl.ds(r, tk), :], m1, a1, c)
            return m0, a0, m1, a1

        def body(t, carry):
            r0 = pl.multiple_of(2 * t * tk, tk)
            r1 = pl.multiple_of(r0 + tk, tk)
            r2 = pl.multiple_of(r1 + tk, tk)
            put(1, r1)
            carry = step(0, r0, carry)
            put(0, r2)
            return step(1, r1, carry)

        put(0, 0)
        paired = (n - 1) // 2
        carry = lax.fori_loop(0, paired, body, init + init)
        for t in range(2 * paired, n):
            if t + 1 < n:
                put((t + 1) % 2, (t + 1) * tk)
            carry = step(t % 2, t * tk, carry)
        finish(carry[1], carry[3])


def _kv_head(h):
    group = GQA_HEADS // GQA_KV_HEADS
    return jnp.where(h < MLA_HEADS, h, MLA_HEADS + (h - MLA_HEADS) // group)


def _attention(q, k, v, n_ctx, tile_off):
    B, _, S, _ = q.shape
    nq = S // TQ - tile_off
    tk = next(t for t in range(TK_MAX, 0, -LANES) if S % t == 0)
    kspec = lambda i: pl.BlockSpec((None, None, S, LANES), lambda b, hp, j: (b, _kv_head(2 * hp + i), 0, 0))
    return pl.pallas_call(
        functools.partial(_attn_kernel, tile_off=tile_off, n_ctx=n_ctx, n_all=S, tk=tk),
        grid=(B, N_PAIRS, nq),
        in_specs=[pl.BlockSpec((None, 2, TQ, LANES), lambda b, hp, j: (b, hp, j + tile_off, 0)),
                  kspec(0), kspec(1),
                  pl.BlockSpec((None, 2, S, LANES), lambda b, hp, j: (b, hp, 0, 0))],
        out_specs=pl.BlockSpec((None, TQ, LANES), lambda b, hp, j: (b, j, hp)),
        out_shape=jax.ShapeDtypeStruct((B, nq * TQ, N_PAIRS * LANES), F32),
        scratch_shapes=[pltpu.VMEM((2, 2, TQ, tk), F32)],
        compiler_params=_cparams(("parallel", "parallel", "arbitrary")),
        name="attention",
    )(q, k, k, v)


def _mix_kernel(o_ref, up_ref, uc_ref, un_ref, x_ref, mod_ref, cw_ref, cb_ref, lg_ref, lb_ref,
                mg_ref, wout_ref, n2_ref, rw_ref,
                xo_ref, h_ref, aff_ref, ext_ref, *, tile_off, last_tile):
    j = pl.program_id(1) + tile_off
    has_prev = j >= 2
    has_next = (j >= 1) & (j < last_tile)
    ext_ref[0:HALO, :] = jnp.where(has_prev, up_ref[TM - HALO:TM, :], 0.0)
    ext_ref[HALO:HALO + TM, :] = uc_ref[...]
    ext_ref[HALO + TM:HALO + TM + HALO, :] = jnp.where(has_next, un_ref[0:HALO, :], 0.0)
    ext = ext_ref[...]
    r0 = HALO - CONV_WIDTH // 2
    y = jnp.zeros((TM, CONV_CH), F32) + cb_ref[...]
    for phase in range(SUBLANES):
        shifted = ext if phase == 0 else pltpu.roll(ext, TM + 2 * HALO - phase, 0)
        for r in range(phase, r0 + CONV_WIDTH, SUBLANES):
            if r >= r0:
                y = y + shifted[r - phase:r - phase + TM, :] * cw_ref[r - r0:r - r0 + 1, :]
    mu = jnp.mean(y, axis=-1, keepdims=True)
    yc = y - mu
    yn = yc * lax.rsqrt(jnp.mean(yc * yc, axis=-1, keepdims=True) + EPS) * lg_ref[...] + lb_ref[...]
    o_conv = yn * jax.nn.sigmoid(yn)

    o = o_ref[...]
    w_mla = MLA_HEADS * MLA_V
    w_att = w_mla + GQA_HEADS * GQA_HEAD_DIM
    cat = jnp.concatenate([
        _rms(o[:, :w_mla], mg_ref[:, :w_mla]).astype(BF16),
        _rms(o[:, w_mla:w_att], mg_ref[:, w_mla:w_att]).astype(BF16),
        _rms(o_conv, mg_ref[:, w_att:]).astype(BF16)], axis=-1)
    mix = jnp.dot(cat, wout_ref[...], preferred_element_type=F32)
    x = x_ref[...] + mod_ref[2:3, :] * mix
    xo_ref[...] = x
    hm = _rms(x, n2_ref[...]) * (1.0 + mod_ref[4:5, :]) + mod_ref[3:4, :]
    h_ref[...] = hm
    logits = lax.dot_general(rw_ref[...], hm, (((1,), (1,)), ((), ())),
                             preferred_element_type=F32, precision=HIGHEST)
    e = jnp.exp(logits - jnp.max(logits, axis=0, keepdims=True))
    aff_ref[...] = e / jnp.sum(e, axis=0, keepdims=True)


def _mix(o, u, xs, mod, lw, tile_off):
    B, S_o, _ = o.shape
    D = xs.shape[-1]
    nt_all = u.shape[1] // TM
    nt = S_o // TM
    const2 = lambda b, j: (0, 0)
    weights = [lw["conv_w"], lw["conv_dw_b"], lw["conv_ln_g"], lw["conv_ln_b"], lw["mix_out_g"], lw["w_out"],
               lw["norm2_g"], lw["router_wt"]]
    uspec = lambda d: pl.BlockSpec(
        (None, TM, CONV_CH), lambda b, j: (b, jnp.clip(j + tile_off + d, 0, nt_all - 1), 0))
    return pl.pallas_call(
        functools.partial(_mix_kernel, tile_off=tile_off, last_tile=nt_all - 1),
        grid=(B, nt),
        in_specs=[pl.BlockSpec((None, TM, o.shape[-1]), lambda b, j: (b, j, 0)),
                  uspec(-1), uspec(0), uspec(1),
                  pl.BlockSpec((None, TM, D), lambda b, j: (b, j + tile_off, 0)),
                  pl.BlockSpec((None, 6, D), lambda b, j: (_mod_row(b, j + tile_off), 0, 0))]
                 + [pl.BlockSpec(w.shape, const2) for w in weights],
        out_specs=[pl.BlockSpec((None, TM, D), lambda b, j: (b, j, 0)),
                   pl.BlockSpec((None, TM, D), lambda b, j: (b, j, 0)),
                   pl.BlockSpec((None, N_EXPERTS, TM), lambda b, j: (b, 0, j))],
        out_shape=[jax.ShapeDtypeStruct((B, S_o, D), F32),
                   jax.ShapeDtypeStruct((B, S_o, D), F32),
                   jax.ShapeDtypeStruct((B, N_EXPERTS, S_o), F32)],
        scratch_shapes=[pltpu.VMEM((TM + 2 * HALO, CONV_CH), F32)],
        compiler_params=_cparams(("parallel", "parallel")),
        name="mix",
    )(o, u, u, u, xs, mod, *weights)


def _moe_kernel(idx_ref, h_hbm, g_ref, w1_ref, w3_ref, w2_ref, y_ref, xbuf0, xbuf1, w1b, w3b, w2b, sem,
                *, nchunk, rows, per_it, r_pad):
    e, f = pl.program_id(0), pl.program_id(1)
    ne, nf = pl.num_programs(0), pl.num_programs(1)
    xbufs = (xbuf0, xbuf1)

    def request(expert, r0, n, into):
        for k in range(n):
            src = idx_ref[expert * r_pad + r0 + k]
            pltpu.make_async_copy(h_hbm.at[pl.ds(src, 1)], xbufs[into].at[pl.ds(r0 + k, 1)], sem.at[into]).start()

    def wait_rows(which):
        pltpu.make_async_copy(h_hbm.at[pl.ds(0, r_pad)], xbufs[which], sem.at[which]).wait()

    @pl.when((e == 0) & (f == 0))
    def _():
        def first(i, carry):
            request(0, pl.multiple_of(i * per_it, 8), per_it, 0)
            return carry
        lax.fori_loop(0, r_pad // per_it, first, 0)

    nxt = jnp.minimum(e + 1, ne - 1)

    def run(cur):
        @pl.when(f == 0)
        def _():
            wait_rows(cur)
            y_ref[...] = jnp.zeros_like(y_ref)

        w1b[...] = w1_ref[...].astype(BF16)
        w3b[...] = w3_ref[...].astype(BF16)
        w2b[...] = w2_ref[...].astype(BF16)
        for c in range(nchunk):
            request(nxt, pl.multiple_of((f * nchunk + c) * per_it, 8), per_it, 1 - cur)
            xb = xbufs[cur][c * rows:(c + 1) * rows, :].astype(BF16)
            h1 = jnp.dot(xb, w1b[...], preferred_element_type=F32)
            h3 = jnp.dot(xb, w3b[...], preferred_element_type=F32)
            hid = (h1 * jax.nn.sigmoid(h1) * h3).astype(BF16)
            y_ref[c * rows:(c + 1) * rows, :] += jnp.dot(hid, w2b[...], preferred_element_type=F32)

        @pl.when((e == ne - 1) & (f == nf - 1))
        def _():
            wait_rows(1 - cur)

    for parity in (0, 1):
        pl.when(e % 2 == parity)(functools.partial(run, parity))

    @pl.when(f == nf - 1)
    def _():
        y_ref[...] = y_ref[...] * g_ref[...]


def _moe_tiles(R, FF):
    rows = next(r for r in range(1056, 15, -16) if R % r == 0)
    tf = next(t for t in range(512, LANES - 1, -LANES) if FF % t == 0)
    steps = (R // rows) * (FF // tf)
    per_it = 8 * (-(-R // (8 * steps)))
    return rows, tf, per_it, per_it * steps


def _experts(h_flat, gidx, gate, w1, w3, w2, l):
    E, R = gidx.shape
    D = h_flat.shape[-1]
    FF = w1.shape[-1]
    rows, tf, per_it, r_pad = _moe_tiles(R, FF)
    idx = jnp.pad(gidx, ((0, 0), (0, r_pad - R))).reshape(E * r_pad)
    grid_spec = pltpu.PrefetchScalarGridSpec(
        num_scalar_prefetch=1,
        grid=(E, FF // tf),
        in_specs=[pl.BlockSpec(memory_space=pl.ANY),
                  pl.BlockSpec((None, R, 1), lambda e, f, idx: (e, 0, 0)),
                  pl.BlockSpec((None, None, D, tf), lambda e, f, idx: (l, e, 0, f)),
                  pl.BlockSpec((None, None, D, tf), lambda e, f, idx: (l, e, 0, f)),
                  pl.BlockSpec((None, None, tf, D), lambda e, f, idx: (l, e, f, 0))],
        out_specs=pl.BlockSpec((None, R, D), lambda e, f, idx: (e, 0, 0)),
        scratch_shapes=[pltpu.VMEM((r_pad, D), F32), pltpu.VMEM((r_pad, D), F32),
                        pltpu.VMEM((D, tf), BF16), pltpu.VMEM((D, tf), BF16), pltpu.VMEM((tf, D), BF16),
                        pltpu.SemaphoreType.DMA((2,))])
    return pl.pallas_call(
        functools.partial(_moe_kernel, nchunk=R // rows, rows=rows, per_it=per_it, r_pad=r_pad),
        grid_spec=grid_spec,
        out_shape=jax.ShapeDtypeStruct((E, R, D), F32),
        compiler_params=_cparams(("arbitrary", "arbitrary")),
        name="experts",
    )(idx, h_flat, gate, w1, w3, w2)


COMBINE_CHUNK = 256


def _combine_kernel(src_ref, dst_ref, y_hbm, out_hbm, acc, stage0, stage1, stage2, sem, osem, *, n_entries):
    b = pl.program_id(0)
    stages = (stage0, stage1, stage2)
    nchunks = n_entries // COMBINE_CHUNK
    groups = COMBINE_CHUNK // SUBLANES
    acc[...] = jnp.zeros_like(acc)

    def request(c, row, slot):
        src = src_ref[b * n_entries + c * COMBINE_CHUNK + row]
        pltpu.make_async_copy(y_hbm.at[pl.ds(src, 1)], stages[slot].at[pl.ds(row, 1)], sem.at[slot]).start()

    def wait_chunk(slot):
        pltpu.make_async_copy(y_hbm.at[pl.ds(0, COMBINE_CHUNK)], stages[slot], sem.at[slot]).wait()

    for c0 in range(2):
        def first(i, carry, c0=c0):
            for k in range(SUBLANES):
                request(c0, pl.multiple_of(i * SUBLANES, SUBLANES) + k, c0)
            return carry
        lax.fori_loop(0, groups, first, 0)

    def add_chunk(c, slot):
        wait_chunk(slot)
        nxt = jnp.minimum(c + 2, nchunks - 1)
        base = b * n_entries + c * COMBINE_CHUNK

        def group(i, carry):
            r0 = pl.multiple_of(i * SUBLANES, SUBLANES)
            for k in range(SUBLANES):
                request(nxt, r0 + k, (slot + 2) % 3)
            dsts = [dst_ref[base + r0 + k] for k in range(SUBLANES)]
            rows = [acc[pl.ds(d, 1), :] + stages[slot][pl.ds(r0 + k, 1), :] for k, d in enumerate(dsts)]
            for d, row in zip(dsts, rows):
                acc[pl.ds(d, 1), :] = row
            return carry
        lax.fori_loop(0, groups, group, 0)

    def triple(t, carry):
        for s in range(3):
            add_chunk(3 * t + s, s)
        return carry
    lax.fori_loop(0, nchunks // 3, triple, 0)
    for c in range(nchunks - nchunks % 3, nchunks):
        add_chunk(c, c % 3)
    wait_chunk(nchunks % 3)
    wait_chunk((nchunks + 1) % 3)
    done = pltpu.make_async_copy(acc, out_hbm.at[b], osem)
    done.start()
    done.wait()


def _combine(y_flat, src, dst, B, S_o):
    D = y_flat.shape[-1]
    n_entries = src.shape[0] // B
    assert n_entries % COMBINE_CHUNK == 0 and n_entries >= 3 * COMBINE_CHUNK
    grid_spec = pltpu.PrefetchScalarGridSpec(
        num_scalar_prefetch=2,
        grid=(B,),
        in_specs=[pl.BlockSpec(memory_space=pl.ANY)],
        out_specs=pl.BlockSpec(memory_space=pl.ANY),
        scratch_shapes=[pltpu.VMEM((S_o, D), F32),
                        pltpu.VMEM((COMBINE_CHUNK, D), F32), pltpu.VMEM((COMBINE_CHUNK, D), F32),
                        pltpu.VMEM((COMBINE_CHUNK, D), F32),
                        pltpu.SemaphoreType.DMA((3,)), pltpu.SemaphoreType.DMA])
    return pl.pallas_call(
        functools.partial(_combine_kernel, n_entries=n_entries),
        grid_spec=grid_spec,
        out_shape=jax.ShapeDtypeStruct((B, S_o, D), F32),
        compiler_params=_cparams(("arbitrary",)),
        name="combine",
    )(src, dst, y_flat)


def _final_kernel(x_ref, moe_ref, mod_ref, g_ref, o_ref):
    x = x_ref[...] + mod_ref[5:6, :] * moe_ref[...]
    o_ref[...] = _rms(x, g_ref[...])


def _final(x, moe, mod, g):
    B, N, D = x.shape
    tok = lambda b, j: (b, j, 0)
    return pl.pallas_call(
        _final_kernel,
        grid=(B, N // TM),
        in_specs=[pl.BlockSpec((None, TM, D), tok), pl.BlockSpec((None, TM, D), tok),
                  pl.BlockSpec((None, 6, D), lambda b, j: (b, 0, 0)),
                  pl.BlockSpec((1, D), lambda b, j: (0, 0))],
        out_specs=pl.BlockSpec((None, TM, D), tok),
        out_shape=jax.ShapeDtypeStruct((B, N, D), F32),
        compiler_params=_cparams(("parallel", "parallel")),
        name="final_norm",
    )(x, moe, mod, g)


def _pad_cols(w, lo, width=LANES):
    return jnp.pad(w, ((0, 0), (lo, width - lo - w.shape[1])))


def _prep_layer(p):
    w_in = p["w_in"]
    o = 0
    parts = {}
    for name, size in (("cq", MLA_Q_RANK), ("ckv", MLA_KV_RANK), ("kr", MLA_ROPE),
                       ("zq", GQA_HEADS * GQA_HEAD_DIM), ("zk", GQA_KV_HEADS * GQA_HEAD_DIM),
                       ("zv", GQA_KV_HEADS * GQA_HEAD_DIM), ("zc", 2 * CONV_CH)):
        parts[name] = w_in[:, o:o + size]
        o += size
    hd = GQA_HEAD_DIM
    group = GQA_HEADS // GQA_KV_HEADS
    cols = [parts["cq"], parts["ckv"]]
    cols += [_pad_cols(parts["zq"][:, h * hd:(h + 1) * hd], 0) for h in range(GQA_HEADS)]
    cols += [_pad_cols(parts["zk"][:, h * hd:(h + 1) * hd], 0) for h in range(GQA_KV_HEADS)]
    cols += [_pad_cols(parts["zv"][:, (h // group) * hd:(h // group + 1) * hd], (h % 2) * hd)
             for h in range(GQA_HEADS)]
    cols += [parts["zc"], _pad_cols(parts["kr"], MLA_NOPE)]
    w_in_p = jnp.concatenate(cols, axis=1).astype(BF16)

    dq = MLA_NOPE + MLA_ROPE
    w_uq = jnp.concatenate([_pad_cols(p["mla_w_uq"][:, h * dq:(h + 1) * dq], 0) for h in range(MLA_HEADS)],
                           axis=1).astype(BF16)
    dkv = MLA_NOPE + MLA_V
    wk = [_pad_cols(p["mla_w_ukv"][:, h * dkv:h * dkv + MLA_NOPE], 0) for h in range(MLA_HEADS)]
    wv = [_pad_cols(p["mla_w_ukv"][:, h * dkv + MLA_NOPE:(h + 1) * dkv], (h % 2) * MLA_V) for h in range(MLA_HEADS)]
    w_ukv = jnp.concatenate(wk + wv, axis=1).astype(BF16)

    row = lambda a: a.reshape(1, -1)
    return {
        "norm1_g": row(p["norm1_g"]), "w_in": w_in_p,
        "mla_q_norm_g": row(p["mla_q_norm_g"]), "w_uq": w_uq,
        "mla_kv_norm_g": row(p["mla_kv_norm_g"]), "w_ukv": w_ukv,
        "gqa_q_norm_g": _pad_cols(row(p["gqa_q_norm_g"]), 0), "gqa_k_norm_g": _pad_cols(row(p["gqa_k_norm_g"]), 0),
        "conv_w": jnp.pad(p["conv_dw_w"], ((0, 32 - CONV_WIDTH), (0, 0))),
        "conv_dw_b": row(p["conv_dw_b"]), "conv_ln_g": row(p["conv_ln_g"]), "conv_ln_b": row(p["conv_ln_b"]),
        "mix_out_g": row(p["mix_out_g"]), "w_out": p["w_out"].astype(BF16),
        "norm2_g": row(p["norm2_g"]), "router_wt": p["router_w"].T,
    }


def _rope_tables(n, n_ctx):
    rows = n // GRID_W
    row = jnp.repeat(jnp.arange(rows), GRID_W).astype(F32)
    col = jnp.tile(jnp.arange(GRID_W), rows).astype(F32)
    tabs = []
    for d_rot, lo in ((MLA_ROPE, MLA_NOPE), (GQA_HEAD_DIM, 0)):
        d_axis = d_rot // 2
        inv = ROPE_BASE ** (-jnp.arange(0, d_axis, 2, dtype=F32) / d_axis)
        ang = jnp.concatenate([row[:, None] * inv, col[:, None] * inv], axis=-1)
        cos, sin = jnp.cos(ang), jnp.sin(ang)
        cos_b = jnp.pad(jnp.concatenate([cos, cos], axis=1) - 1.0, ((n_ctx, 0), (lo, LANES - lo - d_rot))) + 1.0
        sin_b = jnp.pad(jnp.concatenate([-sin, sin], axis=1), ((n_ctx, 0), (lo, LANES - lo - d_rot)))
        tabs += [cos_b, sin_b]
    return tabs


def _route(aff, cap):
    gate, idx = lax.top_k(aff, cap)
    return idx.transpose(1, 0, 2), gate.transpose(1, 0, 2)


def _moe_layer(aff, h, n_ctx, with_ctx, w1, w3, w2, l):
    B, S_o, D = h.shape
    E = aff.shape[1]
    lat0 = n_ctx if with_ctx else 0
    sets = [(lat0, S_o - lat0)] + ([(0, n_ctx)] if with_ctx else [])
    tok, gates = [], []
    for row0, n in sets:
        idx, gate = _route(aff[:, :, row0:row0 + n], CAPACITY_FACTOR * n // N_EXPERTS)
        tok.append(idx + row0)
        gates.append(gate)
    caps = [t.shape[-1] for t in tok]
    assert all(cap % SUBLANES == 0 for cap in caps)
    R = B * sum(caps)
    sample = jnp.arange(B, dtype=jnp.int32)[None, :, None]
    gidx = jnp.concatenate([(t + sample * S_o).reshape(E, -1) for t in tok], axis=1)
    gate = jnp.concatenate([g.reshape(E, -1) for g in gates], axis=1)
    y = _experts(h.reshape(B * S_o, D), gidx, gate[..., None], w1, w3, w2, l)
    expert = jnp.arange(E, dtype=jnp.int32)[:, None, None]
    offs = [B * sum(caps[:i]) for i in range(len(caps))]
    src = [expert * R + off + sample * cap + jnp.arange(cap, dtype=jnp.int32) for off, cap in zip(offs, caps)]
    src = jnp.concatenate([jnp.broadcast_to(s_, t.shape) for s_, t in zip(src, tok)], axis=2)
    dst = jnp.concatenate(tok, axis=2)
    to_sample_major = lambda a: a.transpose(1, 0, 2).reshape(-1).astype(jnp.int32)
    return _combine(y.reshape(E * R, D), to_sample_major(src), to_sample_major(dst), B, S_o)


def kernel(x, c, ctx, c_ctx, mod_w, mod_b, norm1_g, w_in, mla_q_norm_g, mla_w_uq, mla_kv_norm_g, mla_w_ukv,
           gqa_q_norm_g, gqa_k_norm_g, conv_dw_w, conv_dw_b, conv_ln_g, conv_ln_b, mix_out_g, w_out, norm2_g,
           router_w, exp_w1, exp_w3, exp_w2, final_g):
    B, N, D = x.shape
    n_ctx = ctx.shape[1]
    depth = mod_w.shape[0]
    assert n_ctx == TM == TQ and N % TM == 0
    stacked = {"norm1_g": norm1_g, "w_in": w_in, "mla_q_norm_g": mla_q_norm_g, "mla_w_uq": mla_w_uq,
               "mla_kv_norm_g": mla_kv_norm_g, "mla_w_ukv": mla_w_ukv, "gqa_q_norm_g": gqa_q_norm_g,
               "gqa_k_norm_g": gqa_k_norm_g, "conv_dw_w": conv_dw_w, "conv_dw_b": conv_dw_b,
               "conv_ln_g": conv_ln_g, "conv_ln_b": conv_ln_b, "mix_out_g": mix_out_g, "w_out": w_out,
               "norm2_g": norm2_g, "router_w": router_w}

    assert B <= CTX_ROW
    cvec = jnp.zeros((MOD_ROWS, D), F32).at[:B].set(c).at[CTX_ROW].set(c_ctx)
    mods = _modulation(cvec, mod_w, mod_b).reshape(depth, MOD_ROWS, 6, D)
    tabs = _rope_tables(N, n_ctx)

    xs = jnp.concatenate([ctx, x], axis=1)
    moe, modp = None, None
    for l in range(depth):
        lw = _prep_layer({k: v[l] for k, v in stacked.items()})
        last = l == depth - 1
        tile_off = 1 if last else 0
        xs, (q, k, v, u) = _project(xs, moe, modp, mods[l], lw, tabs)
        o = _attention(q, k, v, n_ctx, tile_off)
        xs, h, aff = _mix(o, u, xs, mods[l], lw, tile_off)
        moe = _moe_layer(aff, h, n_ctx, not last, exp_w1, exp_w3, exp_w2, l)
        modp = mods[l]
    return _final(xs, moe, modp, final_g.reshape(1, D))
```

```python
import functools
import math

import jax
import jax.numpy as jnp
from jax import lax
from jax.experimental import pallas as pl
from jax.experimental.pallas import tpu as pltpu

F32 = jnp.float32
BF16 = jnp.bfloat16
HIGHEST = lax.Precision.HIGHEST

GRID_W = 64
ROPE_BASE = 10000.0
EPS = 1e-6

MLA_HEADS = 8
MLA_Q_RANK = 256
MLA_KV_RANK = 128
MLA_NOPE = 64
MLA_ROPE = 32
MLA_V = 64
GQA_HEADS = 4
GQA_KV_HEADS = 2
GQA_HEAD_DIM = 64
CONV_CH = 256
CONV_WIDTH = 31
N_EXPERTS = 16
CAPACITY_FACTOR = 2

MLA_SCALE = 1.0 / math.sqrt(MLA_NOPE + MLA_ROPE)
GQA_SCALE = 1.0 / math.sqrt(GQA_HEAD_DIM)

LANES = 128
SUBLANES = 8
N_QHEADS = MLA_HEADS + GQA_HEADS
N_KHEADS = MLA_HEADS + GQA_KV_HEADS
N_PAIRS = N_QHEADS // 2
TM = 256
TQ = 256
TK_MAX = 2816
ONES_LANE = (MLA_V, 0)
LOG2E = math.log2(math.e)
HALO = 16
VMEM_LIMIT = 56 * 1024 * 1024
MOD_ROWS = 8
CTX_ROW = MOD_ROWS - 1

C_CQ = 0
C_CKV = C_CQ + MLA_Q_RANK
C_ZQ = C_CKV + MLA_KV_RANK
C_ZK = C_ZQ + GQA_HEADS * LANES
C_ZV = C_ZK + GQA_KV_HEADS * LANES
C_ZA = C_ZV + GQA_HEADS * LANES
C_ZB = C_ZA + CONV_CH
C_KR = C_ZB + CONV_CH
IN_PAD = C_KR + LANES


def _cparams(sem):
    return pltpu.CompilerParams(dimension_semantics=sem, vmem_limit_bytes=VMEM_LIMIT)


def _rms(x, g):
    return x * lax.rsqrt(jnp.mean(x * x, axis=-1, keepdims=True) + EPS) * g


def _head_rms(x, g, width):
    return x * lax.rsqrt(jnp.sum(x * x, axis=-1, keepdims=True) * (1.0 / width) + EPS) * g


def _rope(x, cos, sin, lo, half):
    lane = lax.broadcasted_iota(jnp.int32, x.shape, 1)
    first = (lane >= lo) & (lane < lo + half)
    rot = jnp.where(first, pltpu.roll(x, LANES - half, 1), pltpu.roll(x, half, 1))
    return x * cos + rot * sin


def _ones_col(h):
    lane = lax.broadcasted_iota(jnp.int32, (1, LANES), 1)
    return (lane == ONES_LANE[h % 2]).astype(F32)


def _mod_kernel(c_ref, w_ref, b_ref, o_ref):
    c = c_ref[...]
    a = c * jax.nn.sigmoid(c)
    o_ref[...] = jnp.dot(a, w_ref[...], preferred_element_type=F32, precision=HIGHEST) + b_ref[...]


def _modulation(cvec, mod_w, mod_b):
    L, D, W = mod_w.shape
    tn = 1536
    return pl.pallas_call(
        _mod_kernel,
        grid=(L, W // tn),
        in_specs=[pl.BlockSpec((MOD_ROWS, D), lambda l, n: (0, 0)),
                  pl.BlockSpec((None, D, tn), lambda l, n: (l, 0, n)),
                  pl.BlockSpec((None, 1, tn), lambda l, n: (l, 0, n))],
        out_specs=pl.BlockSpec((None, MOD_ROWS, tn), lambda l, n: (l, 0, n)),
        out_shape=jax.ShapeDtypeStruct((L, MOD_ROWS, W), F32),
        compiler_params=_cparams(("arbitrary", "arbitrary")),
        name="modulation",
    )(cvec, mod_w, mod_b.reshape(L, 1, W))


def _proj_kernel(*refs, has_moe):
    if has_moe:
        x_ref, moe_ref, modp_ref, refs = refs[0], refs[1], refs[2], refs[3:]
    else:
        x_ref, refs = refs[0], refs[1:]
    (mod_ref, n1_ref, win_ref, gq_ref, wuq_ref, gkv_ref, wukv_ref, ggq_ref, ggk_ref,
     cm_ref, sm_ref, cg_ref, sg_ref) = refs[:13]
    outs = refs[13:]
    if has_moe:
        xo_ref, outs = outs[0], outs[1:]
    q_ref, k_ref, v_ref, u_ref = outs

    x = x_ref[...]
    if has_moe:
        x = x + modp_ref[5:6, :] * moe_ref[...]
        xo_ref[...] = x
    hm = _rms(x, n1_ref[...]) * (1.0 + mod_ref[1:2, :]) + mod_ref[0:1, :]
    z = jnp.dot(hm.astype(BF16), win_ref[...], preferred_element_type=F32)

    cos_m, sin_m, cos_g, sin_g = cm_ref[...], sm_ref[...], cg_ref[...], sg_ref[...]

    cqn = _rms(z[:, C_CQ:C_CQ + MLA_Q_RANK], gq_ref[...])
    q = jnp.dot(cqn.astype(BF16), wuq_ref[...], preferred_element_type=F32)
    for h in range(MLA_HEADS):
        q_ref[h] = _rope(q[:, h * LANES:(h + 1) * LANES], cos_m, sin_m, MLA_NOPE, MLA_ROPE // 2).astype(BF16)

    ckvn = _rms(z[:, C_CKV:C_CKV + MLA_KV_RANK], gkv_ref[...])
    kv = jnp.dot(ckvn.astype(BF16), wukv_ref[...], preferred_element_type=F32)
    kr = _rope(z[:, C_KR:C_KR + LANES], cos_m, sin_m, MLA_NOPE, MLA_ROPE // 2)
    for h in range(MLA_HEADS):
        k_ref[h] = (kv[:, h * LANES:(h + 1) * LANES] + kr).astype(BF16)
        v_ref[h] = (kv[:, (MLA_HEADS + h) * LANES:(MLA_HEADS + h + 1) * LANES] + _ones_col(h)).astype(BF16)

    for h in range(GQA_HEADS):
        zq = z[:, C_ZQ + h * LANES:C_ZQ + (h + 1) * LANES]
        q_ref[MLA_HEADS + h] = _rope(_head_rms(zq, ggq_ref[...], GQA_HEAD_DIM), cos_g, sin_g,
                                     0, GQA_HEAD_DIM // 2).astype(BF16)
        v_ref[MLA_HEADS + h] = (z[:, C_ZV + h * LANES:C_ZV + (h + 1) * LANES] + _ones_col(h)).astype(BF16)
    for h in range(GQA_KV_HEADS):
        zk = z[:, C_ZK + h * LANES:C_ZK + (h + 1) * LANES]
        k_ref[MLA_HEADS + h] = _rope(_head_rms(zk, ggk_ref[...], GQA_HEAD_DIM), cos_g, sin_g,
                                     0, GQA_HEAD_DIM // 2).astype(BF16)

    u_ref[...] = z[:, C_ZA:C_ZA + CONV_CH] * jax.nn.sigmoid(z[:, C_ZB:C_ZB + CONV_CH])


def _mod_row(b, j):
    return jnp.where(j == 0, CTX_ROW, b)


def _project(xs, moe, modp, mod, lw, tabs):
    B, S, D = xs.shape
    nt = S // TM
    has_moe = moe is not None
    tok = lambda b, j: (b, j, 0)
    const2 = lambda b, j: (0, 0)
    modspec = pl.BlockSpec((None, 6, D), lambda b, j: (_mod_row(b, j), 0, 0))
    in_specs = [pl.BlockSpec((None, TM, D), tok)]
    args = [xs]
    if has_moe:
        in_specs += [pl.BlockSpec((None, TM, D), tok), modspec]
        args += [moe, modp]
    weights = [lw["norm1_g"], lw["w_in"], lw["mla_q_norm_g"], lw["w_uq"], lw["mla_kv_norm_g"], lw["w_ukv"],
               lw["gqa_q_norm_g"], lw["gqa_k_norm_g"]]
    in_specs += [modspec] + [pl.BlockSpec(w.shape, const2) for w in weights]
    in_specs += [pl.BlockSpec((TM, LANES), lambda b, j: (j, 0))] * 4
    args += [mod] + weights + list(tabs)
    head_spec = lambda n: pl.BlockSpec((None, n, TM, LANES), lambda b, j: (b, 0, j, 0))
    out_specs = [head_spec(N_QHEADS), head_spec(N_KHEADS), head_spec(N_QHEADS),
                 pl.BlockSpec((None, TM, CONV_CH), tok)]
    out_shape = [jax.ShapeDtypeStruct((B, N_QHEADS, S, LANES), BF16),
                 jax.ShapeDtypeStruct((B, N_KHEADS, S, LANES), BF16),
                 jax.ShapeDtypeStruct((B, N_QHEADS, S, LANES), BF16),
                 jax.ShapeDtypeStruct((B, S, CONV_CH), F32)]
    if has_moe:
        out_specs = [pl.BlockSpec((None, TM, D), tok)] + out_specs
        out_shape = [jax.ShapeDtypeStruct((B, S, D), F32)] + out_shape
    outs = pl.pallas_call(
        functools.partial(_proj_kernel, has_moe=has_moe),
        grid=(B, nt), in_specs=in_specs, out_specs=out_specs, out_shape=out_shape,
        compiler_params=_cparams(("parallel", "parallel")),
        name="project",
    )(*args)
    if has_moe:
        return outs[0], outs[1:]
    return xs, outs


def _scores(q, k):
    return lax.dot_general(q, k, (((1,), (1,)), ((), ())), preferred_element_type=F32)


def _softmax_update(s, v, m, acc, c):
    m_new = jnp.maximum(m, jnp.max(s, axis=-1, keepdims=True))
    alpha = jnp.exp2((m - m_new) * c)
    p = jnp.exp2((s - m_new) * c).astype(BF16)
    acc = alpha * acc + jnp.dot(p, v, preferred_element_type=F32)
    return m_new, acc


def _attn_kernel(q_ref, k0_ref, k1_ref, v_ref, o_ref, s_ref, *, tile_off, n_ctx, n_all, tk):
    hp = pl.program_id(1)
    j = pl.program_id(2) + tile_off
    c = jnp.where(hp < MLA_HEADS // 2, MLA_SCALE * LOG2E, GQA_SCALE * LOG2E).astype(F32)
    q0, q1 = q_ref[0], q_ref[1]
    init = (jnp.full((TQ, 1), -jnp.inf, F32), jnp.zeros((TQ, LANES), F32))

    def finish(acc0, acc1):
        lane = lax.broadcasted_iota(jnp.int32, (TQ, LANES), 1)
        l0 = acc0[:, ONES_LANE[0]:ONES_LANE[0] + 1]
        l1 = acc1[:, ONES_LANE[1]:ONES_LANE[1] + 1]
        o_ref[...] = jnp.where(lane < LANES // 2, acc0 / l0, acc1 / l1)

    @pl.when(j == 0)
    def _():
        _, acc0 = _softmax_update(_scores(q0, k0_ref[0:n_ctx, :]), v_ref[0, 0:n_ctx, :], *init, c)
        _, acc1 = _softmax_update(_scores(q1, k1_ref[0:n_ctx, :]), v_ref[1, 0:n_ctx, :], *init, c)
        finish(acc0, acc1)

    @pl.when(j > 0)
    def _():
        n = n_all // tk

        def put(slot, r):
            s_ref[slot, 0] = _scores(q0, k0_ref[pl.ds(r, tk), :])
            s_ref[slot, 1] = _scores(q1, k1_ref[pl.ds(r, tk), :])

        def step(slot, r, carry):
            m0, a0, m1, a1 = carry
            m0, a0 = _softmax_update(s_ref[slot, 0], v_ref[0, pl.ds(r, tk), :], m0, a0, c)
            m1, a1 = _softmax_update(s_ref[slot, 1], v_ref[1, pl.ds(r, tk), :], m1, a1, c)
            return m0, a0, m1, a1

        def body(t, carry):
            r0 = pl.multiple_of(2 * t * tk, tk)
            r1 = pl.multiple_of(r0 + tk, tk)
            r2 = pl.multiple_of(r1 + tk, tk)
            put(1, r1)
            carry = step(0, r0, carry)
            put(0, r2)
            return step(1, r1, carry)

        put(0, 0)
        paired = (n - 1) // 2
        carry = lax.fori_loop(0, paired, body, init + init)
        for t in range(2 * paired, n):
            if t + 1 < n:
                put((t + 1) % 2, (t + 1) * tk)
            carry = step(t % 2, t * tk, carry)
        finish(carry[1], carry[3])


def _kv_head(h):
    group = GQA_HEADS // GQA_KV_HEADS
    return jnp.where(h < MLA_HEADS, h, MLA_HEADS + (h - MLA_HEADS) // group)


def _attention(q, k, v, n_ctx, tile_off):
    B, _, S, _ = q.shape
    nq = S // TQ - tile_off
    tk = next(t for t in range(TK_MAX, 0, -LANES) if S % t == 0)
    kspec = lambda i: pl.BlockSpec((None, None, S, LANES), lambda b, hp, j: (b, _kv_head(2 * hp + i), 0, 0))
    return pl.pallas_call(
        functools.partial(_attn_kernel, tile_off=tile_off, n_ctx=n_ctx, n_all=S, tk=tk),
        grid=(B, N_PAIRS, nq),
        in_specs=[pl.BlockSpec((None, 2, TQ, LANES), lambda b, hp, j: (b, hp, j + tile_off, 0)),
                  kspec(0), kspec(1),
                  pl.BlockSpec((None, 2, S, LANES), lambda b, hp, j: (b, hp, 0, 0))],
        out_specs=pl.BlockSpec((None, TQ, LANES), lambda b, hp, j: (b, j, hp)),
        out_shape=jax.ShapeDtypeStruct((B, nq * TQ, N_PAIRS * LANES), F32),
        scratch_shapes=[pltpu.VMEM((2, 2, TQ, tk), F32)],
        compiler_params=_cparams(("parallel", "parallel", "arbitrary")),
        name="attention",
    )(q, k, k, v)


def _mix_kernel(o_ref, up_ref, uc_ref, un_ref, x_ref, mod_ref, cw_ref, cb_ref, lg_ref, lb_ref,
                mg_ref, wout_ref, n2_ref, rw_ref,
                xo_ref, h_ref, aff_ref, ext_ref, *, tile_off, last_tile):
    j = pl.program_id(1) + tile_off
    has_prev = j >= 2
    has_next = (j >= 1) & (j < last_tile)
    ext_ref[0:HALO, :] = jnp.where(has_prev, up_ref[TM - HALO:TM, :], 0.0)
    ext_ref[HALO:HALO + TM, :] = uc_ref[...]
    ext_ref[HALO + TM:HALO + TM + HALO, :] = jnp.where(has_next, un_ref[0:HALO, :], 0.0)
    ext = ext_ref[...]
    r0 = HALO - CONV_WIDTH // 2
    y = jnp.zeros((TM, CONV_CH), F32) + cb_ref[...]
    for phase in range(SUBLANES):
        shifted = ext if phase == 0 else pltpu.roll(ext, TM + 2 * HALO - phase, 0)
        for r in range(phase, r0 + CONV_WIDTH, SUBLANES):
            if r >= r0:
                y = y + shifted[r - phase:r - phase + TM, :] * cw_ref[r - r0:r - r0 + 1, :]
    mu = jnp.mean(y, axis=-1, keepdims=True)
    yc = y - mu
    yn = yc * lax.rsqrt(jnp.mean(yc * yc, axis=-1, keepdims=True) + EPS) * lg_ref[...] + lb_ref[...]
    o_conv = yn * jax.nn.sigmoid(yn)

    o = o_ref[...]
    w_mla = MLA_HEADS * MLA_V
    w_att = w_mla + GQA_HEADS * GQA_HEAD_DIM
    cat = jnp.concatenate([
        _rms(o[:, :w_mla], mg_ref[:, :w_mla]).astype(BF16),
        _rms(o[:, w_mla:w_att], mg_ref[:, w_mla:w_att]).astype(BF16),
        _rms(o_conv, mg_ref[:, w_att:]).astype(BF16)], axis=-1)
    mix = jnp.dot(cat, wout_ref[...], preferred_element_type=F32)
    x = x_ref[...] + mod_ref[2:3, :] * mix
    xo_ref[...] = x
    hm = _rms(x, n2_ref[...]) * (1.0 + mod_ref[4:5, :]) + mod_ref[3:4, :]
    h_ref[...] = hm
    logits = lax.dot_general(rw_ref[...], hm, (((1,), (1,)), ((), ())),
                             preferred_element_type=F32, precision=HIGHEST)
    e = jnp.exp(logits - jnp.max(logits, axis=0, keepdims=True))
    aff_ref[...] = e / jnp.sum(e, axis=0, keepdims=True)


def _mix(o, u, xs, mod, lw, tile_off):
    B, S_o, _ = o.shape
    D = xs.shape[-1]
    nt_all = u.shape[1] // TM
    nt = S_o // TM
    const2 = lambda b, j: (0, 0)
    weights = [lw["conv_w"], lw["conv_dw_b"], lw["conv_ln_g"], lw["conv_ln_b"], lw["mix_out_g"], lw["w_out"],
               lw["norm2_g"], lw["router_wt"]]
    uspec = lambda d: pl.BlockSpec(
        (None, TM, CONV_CH), lambda b, j: (b, jnp.clip(j + tile_off + d, 0, nt_all - 1), 0))
    return pl.pallas_call(
        functools.partial(_mix_kernel, tile_off=tile_off, last_tile=nt_all - 1),
        grid=(B, nt),
        in_specs=[pl.BlockSpec((None, TM, o.shape[-1]), lambda b, j: (b, j, 0)),
                  uspec(-1), uspec(0), uspec(1),
                  pl.BlockSpec((None, TM, D), lambda b, j: (b, j + tile_off, 0)),
                  pl.BlockSpec((None, 6, D), lambda b, j: (_mod_row(b, j + tile_off), 0, 0))]
                 + [pl.BlockSpec(w.shape, const2) for w in weights],
        out_specs=[pl.BlockSpec((None, TM, D), lambda b, j: (b, j, 0)),
                   pl.BlockSpec((None, TM, D), lambda b, j: (b, j, 0)),
                   pl.BlockSpec((None, N_EXPERTS, TM), lambda b, j: (b, 0, j))],
        out_shape=[jax.ShapeDtypeStruct((B, S_o, D), F32),
                   jax.ShapeDtypeStruct((B, S_o, D), F32),
                   jax.ShapeDtypeStruct((B, N_EXPERTS, S_o), F32)],
        scratch_shapes=[pltpu.VMEM((TM + 2 * HALO, CONV_CH), F32)],
        compiler_params=_cparams(("parallel", "parallel")),
        name="mix",
    )(o, u, u, u, xs, mod, *weights)


def _moe_kernel(idx_ref, h_hbm, g_ref, w1_ref, w3_ref, w2_ref, y_ref, xbuf0, xbuf1, w1b, w3b, w2b, sem,
                *, nchunk, rows, per_it, r_pad):
    e, f = pl.program_id(0), pl.program_id(1)
    ne, nf = pl.num_programs(0), pl.num_programs(1)
    xbufs = (xbuf0, xbuf1)

    def request(expert, r0, n, into):
        for k in range(n):
            src = idx_ref[expert * r_pad + r0 + k]
            pltpu.make_async_copy(h_hbm.at[pl.ds(src, 1)], xbufs[into].at[pl.ds(r0 + k, 1)], sem.at[into]).start()

    def wait_rows(which):
        pltpu.make_async_copy(h_hbm.at[pl.ds(0, r_pad)], xbufs[which], sem.at[which]).wait()

    @pl.when((e == 0) & (f == 0))
    def _():
        def first(i, carry):
            request(0, pl.multiple_of(i * per_it, 8), per_it, 0)
            return carry
        lax.fori_loop(0, r_pad // per_it, first, 0)

    nxt = jnp.minimum(e + 1, ne - 1)

    def run(cur):
        @pl.when(f == 0)
        def _():
            wait_rows(cur)
            y_ref[...] = jnp.zeros_like(y_ref)

        w1b[...] = w1_ref[...].astype(BF16)
        w3b[...] = w3_ref[...].astype(BF16)
        w2b[...] = w2_ref[...].astype(BF16)
        for c in range(nchunk):
            request(nxt, pl.multiple_of((f * nchunk + c) * per_it, 8), per_it, 1 - cur)
            xb = xbufs[cur][c * rows:(c + 1) * rows, :].astype(BF16)
            h1 = jnp.dot(xb, w1b[...], preferred_element_type=F32)
            h3 = jnp.dot(xb, w3b[...], preferred_element_type=F32)
            hid = (h1 * jax.nn.sigmoid(h1) * h3).astype(BF16)
            y_ref[c * rows:(c + 1) * rows, :] += jnp.dot(hid, w2b[...], preferred_element_type=F32)

        @pl.when((e == ne - 1) & (f == nf - 1))
        def _():
            wait_rows(1 - cur)

    for parity in (0, 1):
        pl.when(e % 2 == parity)(functools.partial(run, parity))

    @pl.when(f == nf - 1)
    def _():
        y_ref[...] = y_ref[...] * g_ref[...]


def _moe_tiles(R, FF):
    rows = next(r for r in range(1056, 15, -16) if R % r == 0)
    tf = next(t for t in range(512, LANES - 1, -LANES) if FF % t == 0)
    steps = (R // rows) * (FF // tf)
    per_it = 8 * (-(-R // (8 * steps)))
    return rows, tf, per_it, per_it * steps


def _experts(h_flat, gidx, gate, w1, w3, w2, l):
    E, R = gidx.shape
    D = h_flat.shape[-1]
    FF = w1.shape[-1]
    rows, tf, per_it, r_pad = _moe_tiles(R, FF)
    idx = jnp.pad(gidx, ((0, 0), (0, r_pad - R))).reshape(E * r_pad)
    grid_spec = pltpu.PrefetchScalarGridSpec(
        num_scalar_prefetch=1,
        grid=(E, FF // tf),
        in_specs=[pl.BlockSpec(memory_space=pl.ANY),
                  pl.BlockSpec((None, R, 1), lambda e, f, idx: (e, 0, 0)),
                  pl.BlockSpec((None, None, D, tf), lambda e, f, idx: (l, e, 0, f)),
                  pl.BlockSpec((None, None, D, tf), lambda e, f, idx: (l, e, 0, f)),
                  pl.BlockSpec((None, None, tf, D), lambda e, f, idx: (l, e, f, 0))],
        out_specs=pl.BlockSpec((None, R, D), lambda e, f, idx: (e, 0, 0)),
        scratch_shapes=[pltpu.VMEM((r_pad, D), F32), pltpu.VMEM((r_pad, D), F32),
                        pltpu.VMEM((D, tf), BF16), pltpu.VMEM((D, tf), BF16), pltpu.VMEM((tf, D), BF16),
                        pltpu.SemaphoreType.DMA((2,))])
    return pl.pallas_call(
        functools.partial(_moe_kernel, nchunk=R // rows, rows=rows, per_it=per_it, r_pad=r_pad),
        grid_spec=grid_spec,
        out_shape=jax.ShapeDtypeStruct((E, R, D), F32),
        compiler_params=_cparams(("arbitrary", "arbitrary")),
        name="experts",
    )(idx, h_flat, gate, w1, w3, w2)


COMBINE_CHUNK = 256


def _combine_kernel(src_ref, dst_ref, y_hbm, out_hbm, acc, stage0, stage1, stage2, sem, osem, *, n_entries):
    b = pl.program_id(0)
    stages = (stage0, stage1, stage2)
    nchunks = n_entries // COMBINE_CHUNK
    groups = COMBINE_CHUNK // SUBLANES
    acc[...] = jnp.zeros_like(acc)

    def request(c, row, slot, lane):
        src = src_ref[b * n_entries + c * COMBINE_CHUNK + row]
        pltpu.make_async_copy(y_hbm.at[pl.ds(src, 1)], stages[slot].at[pl.ds(row, 1)],
                              sem.at[slot]).start(priority=lane % 2)

    def wait_chunk(slot):
        pltpu.make_async_copy(y_hbm.at[pl.ds(0, COMBINE_CHUNK)], stages[slot], sem.at[slot]).wait()

    for c0 in range(2):
        def first(i, carry, c0=c0):
            for k in range(SUBLANES):
                request(c0, pl.multiple_of(i * SUBLANES, SUBLANES) + k, c0, k)
            return carry
        lax.fori_loop(0, groups, first, 0)

    def add_chunk(c, slot):
        wait_chunk(slot)
        nxt = jnp.minimum(c + 2, nchunks - 1)
        base = b * n_entries + c * COMBINE_CHUNK

        def group(i, carry):
            r0 = pl.multiple_of(i * SUBLANES, SUBLANES)
            for k in range(SUBLANES):
                request(nxt, r0 + k, (slot + 2) % 3, k)
            dsts = [dst_ref[base + r0 + k] for k in range(SUBLANES)]
            rows = [acc[pl.ds(d, 1), :] + stages[slot][pl.ds(r0 + k, 1), :] for k, d in enumerate(dsts)]
            for d, row in zip(dsts, rows):
                acc[pl.ds(d, 1), :] = row
            return carry
        lax.fori_loop(0, groups, group, 0)

    def triple(t, carry):
        for s in range(3):
            add_chunk(3 * t + s, s)
        return carry
    lax.fori_loop(0, nchunks // 3, triple, 0)
    for c in range(nchunks - nchunks % 3, nchunks):
        add_chunk(c, c % 3)
    wait_chunk(nchunks % 3)
    wait_chunk((nchunks + 1) % 3)
    done = pltpu.make_async_copy(acc, out_hbm.at[b], osem)
    done.start()
    done.wait()


def _combine(y_flat, src, dst, B, S_o):
    D = y_flat.shape[-1]
    n_entries = src.shape[0] // B
    assert n_entries % COMBINE_CHUNK == 0 and n_entries >= 3 * COMBINE_CHUNK
    grid_spec = pltpu.PrefetchScalarGridSpec(
        num_scalar_prefetch=2,
        grid=(B,),
        in_specs=[pl.BlockSpec(memory_space=pl.ANY)],
        out_specs=pl.BlockSpec(memory_space=pl.ANY),
        scratch_shapes=[pltpu.VMEM((S_o, D), F32),
                        pltpu.VMEM((COMBINE_CHUNK, D), F32), pltpu.VMEM((COMBINE_CHUNK, D), F32),
                        pltpu.VMEM((COMBINE_CHUNK, D), F32),
                        pltpu.SemaphoreType.DMA((3,)), pltpu.SemaphoreType.DMA])
    return pl.pallas_call(
        functools.partial(_combine_kernel, n_entries=n_entries),
        grid_spec=grid_spec,
        out_shape=jax.ShapeDtypeStruct((B, S_o, D), F32),
        compiler_params=_cparams(("arbitrary",)),
        name="combine",
    )(src, dst, y_flat)


def _final_kernel(x_ref, moe_ref, mod_ref, g_ref, o_ref):
    x = x_ref[...] + mod_ref[5:6, :] * moe_ref[...]
    o_ref[...] = _rms(x, g_ref[...])


def _final(x, moe, mod, g):
    B, N, D = x.shape
    tok = lambda b, j: (b, j, 0)
    return pl.pallas_call(
        _final_kernel,
        grid=(B, N // TM),
        in_specs=[pl.BlockSpec((None, TM, D), tok), pl.BlockSpec((None, TM, D), tok),
                  pl.BlockSpec((None, 6, D), lambda b, j: (b, 0, 0)),
                  pl.BlockSpec((1, D), lambda b, j: (0, 0))],
        out_specs=pl.BlockSpec((None, TM, D), tok),
        out_shape=jax.ShapeDtypeStruct((B, N, D), F32),
        compiler_params=_cparams(("parallel", "parallel")),
        name="final_norm",
    )(x, moe, mod, g)


def _pad_cols(w, lo, width=LANES):
    return jnp.pad(w, ((0, 0), (lo, width - lo - w.shape[1])))


def _prep_layer(p):
    w_in = p["w_in"]
    o = 0
    parts = {}
    for name, size in (("cq", MLA_Q_RANK), ("ckv", MLA_KV_RANK), ("kr", MLA_ROPE),
                       ("zq", GQA_HEADS * GQA_HEAD_DIM), ("zk", GQA_KV_HEADS * GQA_HEAD_DIM),
                       ("zv", GQA_KV_HEADS * GQA_HEAD_DIM), ("zc", 2 * CONV_CH)):
        parts[name] = w_in[:, o:o + size]
        o += size
    hd = GQA_HEAD_DIM
    group = GQA_HEADS // GQA_KV_HEADS
    cols = [parts["cq"], parts["ckv"]]
    cols += [_pad_cols(parts["zq"][:, h * hd:(h + 1) * hd], 0) for h in range(GQA_HEADS)]
    cols += [_pad_cols(parts["zk"][:, h * hd:(h + 1) * hd], 0) for h in range(GQA_KV_HEADS)]
    cols += [_pad_cols(parts["zv"][:, (h // group) * hd:(h // group + 1) * hd], (h % 2) * hd)
             for h in range(GQA_HEADS)]
    cols += [parts["zc"], _pad_cols(parts["kr"], MLA_NOPE)]
    w_in_p = jnp.concatenate(cols, axis=1).astype(BF16)

    dq = MLA_NOPE + MLA_ROPE
    w_uq = jnp.concatenate([_pad_cols(p["mla_w_uq"][:, h * dq:(h + 1) * dq], 0) for h in range(MLA_HEADS)],
                           axis=1).astype(BF16)
    dkv = MLA_NOPE + MLA_V
    wk = [_pad_cols(p["mla_w_ukv"][:, h * dkv:h * dkv + MLA_NOPE], 0) for h in range(MLA_HEADS)]
    wv = [_pad_cols(p["mla_w_ukv"][:, h * dkv + MLA_NOPE:(h + 1) * dkv], (h % 2) * MLA_V) for h in range(MLA_HEADS)]
    w_ukv = jnp.concatenate(wk + wv, axis=1).astype(BF16)

    row = lambda a: a.reshape(1, -1)
    return {
        "norm1_g": row(p["norm1_g"]), "w_in": w_in_p,
        "mla_q_norm_g": row(p["mla_q_norm_g"]), "w_uq": w_uq,
        "mla_kv_norm_g": row(p["mla_kv_norm_g"]), "w_ukv": w_ukv,
        "gqa_q_norm_g": _pad_cols(row(p["gqa_q_norm_g"]), 0), "gqa_k_norm_g": _pad_cols(row(p["gqa_k_norm_g"]), 0),
        "conv_w": jnp.pad(p["conv_dw_w"], ((0, 32 - CONV_WIDTH), (0, 0))),
        "conv_dw_b": row(p["conv_dw_b"]), "conv_ln_g": row(p["conv_ln_g"]), "conv_ln_b": row(p["conv_ln_b"]),
        "mix_out_g": row(p["mix_out_g"]), "w_out": p["w_out"].astype(BF16),
        "norm2_g": row(p["norm2_g"]), "router_wt": p["router_w"].T,
    }


def _rope_tables(n, n_ctx):
    rows = n // GRID_W
    row = jnp.repeat(jnp.arange(rows), GRID_W).astype(F32)
    col = jnp.tile(jnp.arange(GRID_W), rows).astype(F32)
    tabs = []
    for d_rot, lo in ((MLA_ROPE, MLA_NOPE), (GQA_HEAD_DIM, 0)):
        d_axis = d_rot // 2
        inv = ROPE_BASE ** (-jnp.arange(0, d_axis, 2, dtype=F32) / d_axis)
        ang = jnp.concatenate([row[:, None] * inv, col[:, None] * inv], axis=-1)
        cos, sin = jnp.cos(ang), jnp.sin(ang)
        cos_b = jnp.pad(jnp.concatenate([cos, cos], axis=1) - 1.0, ((n_ctx, 0), (lo, LANES - lo - d_rot))) + 1.0
        sin_b = jnp.pad(jnp.concatenate([-sin, sin], axis=1), ((n_ctx, 0), (lo, LANES - lo - d_rot)))
        tabs += [cos_b, sin_b]
    return tabs


def _route(aff, cap):
    gate, idx = lax.top_k(aff, cap)
    return idx.transpose(1, 0, 2), gate.transpose(1, 0, 2)


def _moe_layer(aff, h, n_ctx, with_ctx, w1, w3, w2, l):
    B, S_o, D = h.shape
    E = aff.shape[1]
    lat0 = n_ctx if with_ctx else 0
    sets = [(lat0, S_o - lat0)] + ([(0, n_ctx)] if with_ctx else [])
    tok, gates = [], []
    for row0, n in sets:
        idx, gate = _route(aff[:, :, row0:row0 + n], CAPACITY_FACTOR * n // N_EXPERTS)
        tok.append(idx + row0)
        gates.append(gate)
    caps = [t.shape[-1] for t in tok]
    assert all(cap % SUBLANES == 0 for cap in caps)
    R = B * sum(caps)
    sample = jnp.arange(B, dtype=jnp.int32)[None, :, None]
    gidx = jnp.concatenate([(t + sample * S_o).reshape(E, -1) for t in tok], axis=1)
    gate = jnp.concatenate([g.reshape(E, -1) for g in gates], axis=1)
    y = _experts(h.reshape(B * S_o, D), gidx, gate[..., None], w1, w3, w2, l)
    expert = jnp.arange(E, dtype=jnp.int32)[:, None, None]
    offs = [B * sum(caps[:i]) for i in range(len(caps))]
    src = [expert * R + off + sample * cap + jnp.arange(cap, dtype=jnp.int32) for off, cap in zip(offs, caps)]
    src = jnp.concatenate([jnp.broadcast_to(s_, t.shape) for s_, t in zip(src, tok)], axis=2)
    dst = jnp.concatenate(tok, axis=2)
    to_sample_major = lambda a: a.transpose(1, 0, 2).reshape(-1).astype(jnp.int32)
    return _combine(y.reshape(E * R, D), to_sample_major(src), to_sample_major(dst), B, S_o)


def kernel(x, c, ctx, c_ctx, mod_w, mod_b, norm1_g, w_in, mla_q_norm_g, mla_w_uq, mla_kv_norm_g, mla_w_ukv,
           gqa_q_norm_g, gqa_k_norm_g, conv_dw_w, conv_dw_b, conv_ln_g, conv_ln_b, mix_out_g, w_out, norm2_g,
           router_w, exp_w1, exp_w3, exp_w2, final_g):
    B, N, D = x.shape
    n_ctx = ctx.shape[1]
    depth = mod_w.shape[0]
    assert n_ctx == TM == TQ and N % TM == 0
    stacked = {"norm1_g": norm1_g, "w_in": w_in, "mla_q_norm_g": mla_q_norm_g, "mla_w_uq": mla_w_uq,
               "mla_kv_norm_g": mla_kv_norm_g, "mla_w_ukv": mla_w_ukv, "gqa_q_norm_g": gqa_q_norm_g,
               "gqa_k_norm_g": gqa_k_norm_g, "conv_dw_w": conv_dw_w, "conv_dw_b": conv_dw_b,
               "conv_ln_g": conv_ln_g, "conv_ln_b": conv_ln_b, "mix_out_g": mix_out_g, "w_out": w_out,
               "norm2_g": norm2_g, "router_w": router_w}

    assert B <= CTX_ROW
    cvec = jnp.zeros((MOD_ROWS, D), F32).at[:B].set(c).at[CTX_ROW].set(c_ctx)
    mods = _modulation(cvec, mod_w, mod_b).reshape(depth, MOD_ROWS, 6, D)
    tabs = _rope_tables(N, n_ctx)

    xs = jnp.concatenate([ctx, x], axis=1)
    moe, modp = None, None
    for l in range(depth):
        lw = _prep_layer({k: v[l] for k, v in stacked.items()})
        last = l == depth - 1
        tile_off = 1 if last else 0
        xs, (q, k, v, u) = _project(xs, moe, modp, mods[l], lw, tabs)
        o = _attention(q, k, v, n_ctx, tile_off)
        xs, h, aff = _mix(o, u, xs, mods[l], lw, tile_off)
        moe = _moe_layer(aff, h, n_ctx, not last, exp_w1, exp_w3, exp_w2, l)
        modp = mods[l]
    return _final(xs, moe, modp, final_g.reshape(1, D))
```
